```python
import jax, jax.numpy as jnp
from jax import lax
import numpy as np

D_MODEL = 4096
BATCH = 2
SEQ = 8192
DEPTH = 1

GRID_W = 64
EPS = 1e-6
ROPE_BASE = 10000.0
N_BRANCHES = 2

MLA_HEADS = 16
MLA_Q_RANK = 768
MLA_KV_RANK = 512
MLA_NOPE_DIM = 128
MLA_ROPE_DIM = 64
MLA_QK_DIM = MLA_NOPE_DIM + MLA_ROPE_DIM
MLA_V_DIM = 128
MLA_WIDTH = MLA_HEADS * MLA_V_DIM
Q_BLOCK = 128

NA_HEADS = 16
NA_HEAD_DIM = 128
NA_WIDTH = NA_HEADS * NA_HEAD_DIM
NA_KH_MAX = 8
NA_KW = 16

PEER_HEADS = 8
PEER_N_KEYS = 128
PEER_N_EXPERTS = PEER_N_KEYS * PEER_N_KEYS
PEER_KEY_DIM = 256
PEER_HALF = PEER_KEY_DIM // 2
PEER_TOPK = 16
PEER_TOKEN_BLOCK = 64

IN_SIZES = (MLA_Q_RANK, MLA_KV_RANK, MLA_ROPE_DIM, NA_WIDTH, NA_WIDTH, NA_WIDTH)
D_IN = sum(IN_SIZES)
IN_OFFSETS = tuple(int(o) for o in np.cumsum(IN_SIZES)[:-1])

kernel_name = "hybrid_mla_natten_peer_block"


def rmsnorm(x, g):
    xf = x.astype(jnp.float32)
    xf = xf * lax.rsqrt(jnp.mean(xf * xf, axis=-1, keepdims=True) + EPS)
    return (xf * g.astype(jnp.float32)).astype(x.dtype)


def rope_1d(x, pos):
    half = x.shape[-1] // 2
    freqs = ROPE_BASE ** (-jnp.arange(half, dtype=jnp.float32) / half)
    ang = pos.astype(jnp.float32)[:, None] * freqs[None, :]
    ang = ang.reshape((ang.shape[0],) + (1,) * (x.ndim - 3) + (half,))
    cos = jnp.cos(ang).astype(x.dtype)
    sin = jnp.sin(ang).astype(x.dtype)
    x1, x2 = x[..., :half], x[..., half:]
    return jnp.concatenate([x1 * cos - x2 * sin, x1 * sin + x2 * cos], axis=-1)


def rope_2d(x, row, col):
    h = x.shape[-1] // 2
    return jnp.concatenate([rope_1d(x[..., :h], row), rope_1d(x[..., h:], col)], axis=-1)


def mla_branch(c_q, c_kv, k_rope, q_norm, w_uq, kv_norm, w_ukv, row, col):
    B, S, _ = c_q.shape
    q = (rmsnorm(c_q, q_norm) @ w_uq).reshape(B, S, MLA_HEADS, MLA_QK_DIM)
    q = jnp.concatenate([q[..., :MLA_NOPE_DIM], rope_2d(q[..., MLA_NOPE_DIM:], row, col)], axis=-1)
    kv = (rmsnorm(c_kv, kv_norm) @ w_ukv).reshape(B, S, MLA_HEADS, MLA_NOPE_DIM + MLA_V_DIM)
    k_nope, v = kv[..., :MLA_NOPE_DIM], kv[..., MLA_NOPE_DIM:]
    k_pe = rope_2d(k_rope, row, col)
    k = jnp.concatenate([k_nope, jnp.broadcast_to(k_pe[:, :, None, :], (B, S, MLA_HEADS, MLA_ROPE_DIM))], axis=-1)
    scale = MLA_QK_DIM ** -0.5
    n_blocks = S // Q_BLOCK
    qb = q.reshape(B, n_blocks, Q_BLOCK, MLA_HEADS, MLA_QK_DIM).transpose(1, 0, 2, 3, 4)

    def attend(q_blk):
        s = jnp.einsum('bqhd,bkhd->bhqk', q_blk, k).astype(jnp.float32) * scale
        p = jax.nn.softmax(s, axis=-1).astype(v.dtype)
        return jnp.einsum('bhqk,bkhd->bqhd', p, v)

    o = lax.map(attend, qb)
    return o.transpose(1, 0, 2, 3, 4).reshape(B, S, MLA_WIDTH)


def na_branch(q, k, v, rpb):
    B, S, _ = q.shape
    rows = S // GRID_W
    kh = min(NA_KH_MAX, rows)
    q = q.reshape(B, rows, GRID_W, NA_HEADS, NA_HEAD_DIM)
    k = k.reshape(B, rows, GRID_W, NA_HEADS, NA_HEAD_DIM)
    v = v.reshape(B, rows, GRID_W, NA_HEADS, NA_HEAD_DIM)
    cols = jnp.arange(GRID_W)
    col_start = jnp.clip(cols - NA_KW // 2, 0, GRID_W - NA_KW)
    col_idx = col_start[:, None] + jnp.arange(NA_KW)[None, :]
    dc = col_idx - cols[:, None]
    scale = NA_HEAD_DIM ** -0.5

    def row_attend(args):
        r, q_r = args
        rs = jnp.clip(r - kh // 2, 0, rows - kh)
        k_slab = lax.dynamic_slice_in_dim(k, rs, kh, axis=1)
        v_slab = lax.dynamic_slice_in_dim(v, rs, kh, axis=1)
        k_win = k_slab[:, :, col_idx]
        v_win = v_slab[:, :, col_idx]
        dr = rs + jnp.arange(kh) - r
        bias = rpb[:, dr[:, None, None] + (NA_KH_MAX - 1), dc[None, :, :] + (NA_KW - 1)]
        bias = bias.transpose(0, 2, 1, 3).astype(jnp.float32)
        s = jnp.einsum('bwhd,bkwjhd->bhwkj', q_r, k_win).astype(jnp.float32) * scale + bias[None]
        p = jax.nn.softmax(s.reshape(B, NA_HEADS, GRID_W, kh * NA_KW), axis=-1)
        p = p.reshape(B, NA_HEADS, GRID_W, kh, NA_KW).astype(v.dtype)
        return jnp.einsum('bhwkj,bkwjhd->bwhd', p, v_win)

    o = lax.map(row_attend, (jnp.arange(rows), q.transpose(1, 0, 2, 3, 4)))
    return o.transpose(1, 0, 2, 3, 4).reshape(B, S, NA_WIDTH)


def peer_ffn(h, w_query, subkeys, u, v):
    B, S, D = h.shape
    T = B * S
    ht = h.reshape(T, D)
    q = (ht @ w_query).reshape(T, PEER_HEADS, 2, PEER_HALF)
    s = jnp.einsum('thpc,hpnc->thpn', q, subkeys).astype(jnp.float32)
    top_s, top_i = lax.top_k(s, PEER_TOPK)
    cand_s = top_s[:, :, 0, :, None] + top_s[:, :, 1, None, :]
    cand_i = top_i[:, :, 0, :, None] * PEER_N_KEYS + top_i[:, :, 1, None, :]
    cand_s = cand_s.reshape(T, PEER_HEADS, PEER_TOPK * PEER_TOPK)
    cand_i = cand_i.reshape(T, PEER_HEADS, PEER_TOPK * PEER_TOPK)
    sel_s, pos = lax.top_k(cand_s, PEER_TOPK)
    experts = jnp.take_along_axis(cand_i, pos, axis=-1)
    gates = jax.nn.softmax(sel_s, axis=-1).astype(h.dtype)
    nb = T // PEER_TOKEN_BLOCK

    def block(args):
        x_b, e_b, g_b = args
        u_e = u[e_b]
        v_e = v[e_b]
        act = jax.nn.gelu(jnp.einsum('td,thkd->thk', x_b, u_e), approximate=False)
        return jnp.einsum('thk,thkd->td', g_b * act, v_e)

    out = lax.map(block, (ht.reshape(nb, PEER_TOKEN_BLOCK, D),
                          experts.reshape(nb, PEER_TOKEN_BLOCK, PEER_HEADS, PEER_TOPK),
                          gates.reshape(nb, PEER_TOKEN_BLOCK, PEER_HEADS, PEER_TOPK)))
    return out.reshape(B, S, D)


def setup_inputs(seed: int = 0) -> dict:
    key = jax.random.key(seed)
    ks = jax.random.split(key, 24)
    f = jnp.float32

    def nrm(k, shape, scale):
        return jax.random.normal(k, shape, f) * scale

    def gain(k, n):
        return 1.0 + 0.01 * jax.random.normal(k, (DEPTH, n), f)

    return {
        "x": jax.random.normal(ks[0], (BATCH, SEQ, D_MODEL), f),
        "attn_norm": gain(ks[1], D_MODEL),
        "w_in": nrm(ks[2], (DEPTH, D_MODEL, D_IN), D_MODEL ** -0.5),
        "q_norm": gain(ks[3], MLA_Q_RANK),
        "w_uq": nrm(ks[4], (DEPTH, MLA_Q_RANK, MLA_HEADS * MLA_QK_DIM), MLA_Q_RANK ** -0.5),
        "kv_norm": gain(ks[5], MLA_KV_RANK),
        "w_ukv": nrm(ks[6], (DEPTH, MLA_KV_RANK, MLA_HEADS * (MLA_NOPE_DIM + MLA_V_DIM)), MLA_KV_RANK ** -0.5),
        "w_branch_mla": nrm(ks[7], (DEPTH, MLA_WIDTH, D_MODEL), MLA_WIDTH ** -0.5),
        "na_rpb": nrm(ks[8], (DEPTH, NA_HEADS, 2 * NA_KH_MAX - 1, 2 * NA_KW - 1), 0.02),
        "w_branch_na": nrm(ks[9], (DEPTH, NA_WIDTH, D_MODEL), NA_WIDTH ** -0.5),
        "w_gate": nrm(ks[10], (DEPTH, D_MODEL, N_BRANCHES * D_MODEL), D_MODEL ** -0.5),
        "b_gate": nrm(ks[11], (DEPTH, N_BRANCHES * D_MODEL), 0.01),
        "w_out": nrm(ks[12], (DEPTH, D_MODEL, D_MODEL), D_MODEL ** -0.5),
        "ffn_norm": gain(ks[13], D_MODEL),
        "peer_w_query": nrm(ks[14], (DEPTH, D_MODEL, PEER_HEADS * PEER_KEY_DIM), D_MODEL ** -0.5),
        "peer_subkeys": nrm(ks[15], (DEPTH, PEER_HEADS, 2, PEER_N_KEYS, PEER_HALF), PEER_HALF ** -0.5),
        "peer_u": nrm(ks[16], (DEPTH, PEER_N_EXPERTS, D_MODEL), D_MODEL ** -0.5),
        "peer_v": nrm(ks[17], (DEPTH, PEER_N_EXPERTS, D_MODEL), PEER_HEADS ** -0.5),
        "final_norm": 1.0 + 0.01 * jax.random.normal(ks[18], (D_MODEL,), f),
    }


def reference(x, attn_norm, w_in, q_norm, w_uq, kv_norm, w_ukv, w_branch_mla, na_rpb, w_branch_na,
              w_gate, b_gate, w_out, ffn_norm, peer_w_query, peer_subkeys, peer_u, peer_v, final_norm):
    B, S, D = x.shape
    t = jnp.arange(S)
    row, col = t // GRID_W, t % GRID_W
    for l in range(DEPTH):
        h = rmsnorm(x, attn_norm[l])
        proj = h @ w_in[l]
        c_q, c_kv, k_rope, na_q, na_k, na_v = jnp.split(proj, IN_OFFSETS, axis=-1)
        y_a = mla_branch(c_q, c_kv, k_rope, q_norm[l], w_uq[l], kv_norm[l], w_ukv[l], row, col) @ w_branch_mla[l]
        y_b = na_branch(na_q, na_k, na_v, na_rpb[l]) @ w_branch_na[l]
        g = jax.nn.sigmoid(h @ w_gate[l] + b_gate[l]).reshape(B, S, N_BRANCHES, D)
        merged = g[:, :, 0, :] * y_a + g[:, :, 1, :] * y_b
        x = x + merged @ w_out[l]
        h2 = rmsnorm(x, ffn_norm[l])
        x = x + peer_ffn(h2, peer_w_query[l], peer_subkeys[l], peer_u[l], peer_v[l])
    return rmsnorm(x, final_norm)
```

```python
import functools
import math

import jax
import jax.numpy as jnp
import numpy as np
from jax import lax
from jax.experimental import pallas as pl
from jax.experimental.pallas import tpu as pltpu

F32 = jnp.float32
BF16 = jnp.bfloat16

EPS = 1e-6
ROPE_BASE = 10000.0
GRID_W = 64
NOPE = 128
ROPE = 64
MLA_V = 128
MLA_QK = NOPE + ROPE
QK_PAD = 256
NA_D = 128
NA_KH = 8
NA_KW = 16
NA_ROWS_PER_BLOCK = 8
NA_KEY_ROWS = 16
N_KEYS = 128
PEER_TOPK = 16
LANE = 128
VMEM_LIMIT = 56 * 1024 * 1024

NT_DIMS = (((1,), (1,)), ((), ()))


def _params(*sem):
    return pltpu.CompilerParams(dimension_semantics=sem, vmem_limit_bytes=VMEM_LIMIT)


def _tile(n, pref):
    t = min(n, pref)
    while n % t:
        t //= 2
    return t


def _rms(x, g):
    return x * lax.rsqrt(jnp.mean(x * x, axis=-1, keepdims=True) + EPS) * g


def _rmsnorm_kernel(x_ref, g_ref, o_ref):
    o_ref[...] = _rms(x_ref[...], g_ref[...]).astype(o_ref.dtype)


def _rmsnorm_cast(x, g, tm=256):
    m, d = x.shape
    tm = _tile(m, tm)
    return pl.pallas_call(
        _rmsnorm_kernel,
        grid=(m // tm,),
        in_specs=[pl.BlockSpec((tm, d), lambda i: (i, 0)),
                  pl.BlockSpec((1, d), lambda i: (0, 0))],
        out_specs=pl.BlockSpec((tm, d), lambda i: (i, 0)),
        out_shape=jax.ShapeDtypeStruct((m, d), BF16),
        compiler_params=_params("parallel"),
        name="rmsnorm_cast",
    )(x, g.reshape(1, d))


def _mm_kernel(a_ref, w_ref, o_ref):
    o_ref[...] = jnp.dot(a_ref[...], w_ref[...], preferred_element_type=F32).astype(o_ref.dtype)


def _mm_res_kernel(a_ref, w_ref, r_ref, o_ref):
    acc = jnp.dot(a_ref[...], w_ref[...], preferred_element_type=F32)
    o_ref[...] = (r_ref[...] + acc).astype(o_ref.dtype)


def _matmul(a, w, out_dtype, tm, tn, residual=None, name="matmul"):
    m, k = a.shape
    n = w.shape[1]
    tm, tn = _tile(m, tm), _tile(n, tn)
    in_specs = [pl.BlockSpec((tm, k), lambda i, j: (i, 0)),
                pl.BlockSpec((k, tn), lambda i, j: (0, j))]
    args = [a, w]
    body = _mm_kernel
    if residual is not None:
        in_specs.append(pl.BlockSpec((tm, tn), lambda i, j: (i, j)))
        args.append(residual)
        body = _mm_res_kernel
    return pl.pallas_call(
        body,
        grid=(m // tm, n // tn),
        in_specs=in_specs,
        out_specs=pl.BlockSpec((tm, tn), lambda i, j: (i, j)),
        out_shape=jax.ShapeDtypeStruct((m, n), out_dtype),
        compiler_params=_params("parallel", "parallel"),
        name=name,
    )(*args)


def _rope_pair(blk, cos, sin):
    return blk * cos + pltpu.roll(blk, ROPE, 1) * sin


def _mla_q_kernel(c_ref, g_ref, w_ref, cos_ref, sin_ref, o_ref, *, heads, scale):
    cn = _rms(c_ref[...], g_ref[...]).astype(BF16)
    cos, sin = cos_ref[...], sin_ref[...]
    for h in range(heads):
        y = jnp.dot(cn, w_ref[:, h * QK_PAD:(h + 1) * QK_PAD], preferred_element_type=F32) * scale
        o_ref[:, h * QK_PAD:h * QK_PAD + NOPE] = y[:, :NOPE].astype(o_ref.dtype)
        o_ref[:, h * QK_PAD + NOPE:(h + 1) * QK_PAD] = _rope_pair(y[:, NOPE:], cos, sin).astype(o_ref.dtype)


def _mla_q(lat, g, wq, cos, sin, heads, seq, tm=512):
    t = lat.shape[0]
    r = g.shape[0]
    tm = _tile(seq, tm)
    sb = seq // tm
    return pl.pallas_call(
        functools.partial(_mla_q_kernel, heads=heads, scale=MLA_QK ** -0.5),
        grid=(t // tm,),
        in_specs=[pl.BlockSpec((tm, r), lambda i: (i, 0)),
                  pl.BlockSpec((1, r), lambda i: (0, 0)),
                  pl.BlockSpec((r, heads * QK_PAD), lambda i: (0, 0)),
                  pl.BlockSpec((tm, LANE), lambda i: (i % sb, 0)),
                  pl.BlockSpec((tm, LANE), lambda i: (i % sb, 0))],
        out_specs=pl.BlockSpec((tm, heads * QK_PAD), lambda i: (i, 0)),
        out_shape=jax.ShapeDtypeStruct((t, heads * QK_PAD), BF16),
        compiler_params=_params("parallel"),
        name="mla_q_proj",
    )(lat, g.reshape(1, r), wq, cos, sin)


def _mla_kv_kernel(c_ref, kr_ref, g_ref, wk_ref, wv_ref, cos_ref, sin_ref, k_ref, v_ref, *, heads):
    cn = _rms(c_ref[...], g_ref[...]).astype(BF16)
    kpe = _rope_pair(kr_ref[...], cos_ref[...], sin_ref[...]).astype(k_ref.dtype)
    kn = jnp.dot(cn, wk_ref[...], preferred_element_type=F32).astype(k_ref.dtype)
    v_ref[...] = jnp.dot(cn, wv_ref[...], preferred_element_type=F32).astype(v_ref.dtype)
    for h in range(heads):
        k_ref[:, h * QK_PAD:h * QK_PAD + NOPE] = kn[:, h * NOPE:(h + 1) * NOPE]
        k_ref[:, h * QK_PAD + NOPE:(h + 1) * QK_PAD] = kpe


def _mla_kv(lat, ckv_blk, kr_blk, g, wk, wv, cos, sin, heads, seq, tm=512):
    t = lat.shape[0]
    r = g.shape[0]
    tm = _tile(seq, tm)
    sb = seq // tm
    return pl.pallas_call(
        functools.partial(_mla_kv_kernel, heads=heads),
        grid=(t // tm,),
        in_specs=[pl.BlockSpec((tm, r), lambda i: (i, ckv_blk)),
                  pl.BlockSpec((tm, LANE), lambda i: (i, kr_blk)),
                  pl.BlockSpec((1, r), lambda i: (0, 0)),
                  pl.BlockSpec((r, heads * NOPE), lambda i: (0, 0)),
                  pl.BlockSpec((r, heads * MLA_V), lambda i: (0, 0)),
                  pl.BlockSpec((tm, LANE), lambda i: (i % sb, 0)),
                  pl.BlockSpec((tm, LANE), lambda i: (i % sb, 0))],
        out_specs=[pl.BlockSpec((tm, heads * QK_PAD), lambda i: (i, 0)),
                   pl.BlockSpec((tm, heads * MLA_V), lambda i: (i, 0))],
        out_shape=[jax.ShapeDtypeStruct((t, heads * QK_PAD), BF16),
                   jax.ShapeDtypeStruct((t, heads * MLA_V), BF16)],
        compiler_params=_params("parallel"),
        name="mla_kv_proj",
    )(lat, lat, g.reshape(1, r), wk, wv, cos, sin)


def _mla_attn_kernel(q_ref, k_ref, v_ref, o_ref, *, tk):
    q = q_ref[...]
    tq = q.shape[0]
    n_chunks = k_ref.shape[0] // tk

    def body(j, carry):
        m, l, acc = carry
        off = pl.multiple_of(j * tk, tk)
        s = lax.dot_general(q, k_ref[pl.ds(off, tk), :], NT_DIMS, preferred_element_type=F32)
        m_new = jnp.maximum(m, jnp.max(s, axis=1, keepdims=True))
        alpha = jnp.exp(m - m_new)
        p = jnp.exp(s - m_new)
        l = alpha * l + jnp.sum(p, axis=1, keepdims=True)
        acc = alpha * acc + jnp.dot(p.astype(BF16), v_ref[pl.ds(off, tk), :], preferred_element_type=F32)
        return m_new, l, acc

    init = (jnp.full((tq, 1), -jnp.inf, F32), jnp.zeros((tq, 1), F32), jnp.zeros((tq, MLA_V), F32))
    _, l, acc = lax.fori_loop(0, n_chunks, body, init)
    o_ref[...] = (acc / l).astype(o_ref.dtype)


def _mla_attn(q, k, v, batch, seq, heads, tq=512, tk=512):
    t = q.shape[0]
    tq, tk = _tile(seq, tq), _tile(seq, tk)
    qb = seq // tq
    return pl.pallas_call(
        functools.partial(_mla_attn_kernel, tk=tk),
        grid=(batch, heads, qb),
        in_specs=[pl.BlockSpec((tq, QK_PAD), lambda b, h, i: (b * qb + i, h)),
                  pl.BlockSpec((seq, QK_PAD), lambda b, h, i: (b, h)),
                  pl.BlockSpec((seq, MLA_V), lambda b, h, i: (b, h))],
        out_specs=pl.BlockSpec((tq, MLA_V), lambda b, h, i: (b * qb + i, h)),
        out_shape=jax.ShapeDtypeStruct((t, heads * MLA_V), BF16),
        compiler_params=_params("parallel", "parallel", "parallel"),
        name="mla_attention",
    )(q, k, v)


def _na_block_layout(rows):
    nb = rows // NA_ROWS_PER_BLOCK
    kh = min(NA_KH, rows)
    classes = []
    for blk in (0, 1, nb - 1):
        r0 = blk * NA_ROWS_PER_BLOCK
        ks = min(max(r0 - NA_KH // 2, 0), rows - NA_KEY_ROWS)
        entries = []
        for rq in range(NA_ROWS_PER_BLOCK):
            r = r0 + rq
            rs = min(max(r - kh // 2, 0), rows - kh)
            for j in range(kh):
                entries.append((rq, rs + j - ks, rs + j - r + NA_KH - 1))
        classes.append(entries)
    return classes


def _na_attn_kernel(q_ref, k_ref, v_ref, toep_ref, o_ref, bias_ref, *, rows, scale):
    b, blk = pl.program_id(1), pl.program_id(2)
    nb = rows // NA_ROWS_PER_BLOCK

    @pl.when((b == 0) & (blk == 0))
    def _build_bias():
        bias_ref[...] = jnp.full(bias_ref.shape, -jnp.inf, F32)
        for c, entries in enumerate(_na_block_layout(rows)):
            for rq, rk, d in entries:
                bias_ref[c, rq * GRID_W:(rq + 1) * GRID_W, rk * GRID_W:(rk + 1) * GRID_W] = toep_ref[0, d]

    r0 = blk * NA_ROWS_PER_BLOCK
    ks = jnp.clip(r0 - NA_KH // 2, 0, rows - NA_KEY_ROWS)
    off = pl.multiple_of(ks * GRID_W, GRID_W)
    nk = NA_KEY_ROWS * GRID_W
    cls = jnp.where(blk == 0, 0, jnp.where(blk == nb - 1, 2, 1))
    s = lax.dot_general(q_ref[...], k_ref[pl.ds(off, nk), :], NT_DIMS, preferred_element_type=F32)
    s = s * scale + bias_ref[cls]
    m = jnp.max(s, axis=1, keepdims=True)
    p = jnp.exp(s - m)
    l = jnp.sum(p, axis=1, keepdims=True)
    o = jnp.dot(p.astype(BF16), v_ref[pl.ds(off, nk), :], preferred_element_type=F32)
    o_ref[...] = (o / l).astype(o_ref.dtype)


def _na_attn(qkv, toep, batch, seq, heads):
    t = qkv.shape[0]
    rows = seq // GRID_W
    nb = rows // NA_ROWS_PER_BLOCK
    tq = NA_ROWS_PER_BLOCK * GRID_W
    return pl.pallas_call(
        functools.partial(_na_attn_kernel, rows=rows, scale=NA_D ** -0.5),
        grid=(heads, batch, nb),
        in_specs=[pl.BlockSpec((tq, NA_D), lambda h, b, i: (b * nb + i, h)),
                  pl.BlockSpec((seq, NA_D), lambda h, b, i: (b, heads + h)),
                  pl.BlockSpec((seq, NA_D), lambda h, b, i: (b, 2 * heads + h)),
                  pl.BlockSpec((1, 2 * NA_KH - 1, GRID_W, GRID_W), lambda h, b, i: (h, 0, 0, 0))],
        out_specs=pl.BlockSpec((tq, NA_D), lambda h, b, i: (b * nb + i, h)),
        out_shape=jax.ShapeDtypeStruct((t, heads * NA_D), BF16),
        scratch_shapes=[pltpu.VMEM((3, tq, NA_KEY_ROWS * GRID_W), F32)],
        compiler_params=_params("arbitrary", "arbitrary", "arbitrary"),
        name="na_attention",
    )(qkv, qkv, qkv, toep)


def _gate_merge_kernel(h_ref, wga_ref, wgb_ref, ba_ref, bb_ref, oa_ref, pa_ref, ob_ref, pb_ref, o_ref):
    h = h_ref[...]
    ga = jax.nn.sigmoid(jnp.dot(h, wga_ref[...], preferred_element_type=F32) + ba_ref[...])
    ya = jnp.dot(oa_ref[...], pa_ref[...], preferred_element_type=F32)
    acc = ga * ya
    gb = jax.nn.sigmoid(jnp.dot(h, wgb_ref[...], preferred_element_type=F32) + bb_ref[...])
    yb = jnp.dot(ob_ref[...], pb_ref[...], preferred_element_type=F32)
    o_ref[...] = (acc + gb * yb).astype(o_ref.dtype)


def _gate_merge(h, w_gate, b_gate, oa, pa, ob, pb, tm=512, tn=512):
    m, d = h.shape
    wa, wb = oa.shape[1], ob.shape[1]
    tm, tn = _tile(m, tm), _tile(d, tn)
    nj = d // tn
    return pl.pallas_call(
        _gate_merge_kernel,
        grid=(m // tm, nj),
        in_specs=[pl.BlockSpec((tm, d), lambda i, j: (i, 0)),
                  pl.BlockSpec((d, tn), lambda i, j: (0, j)),
                  pl.BlockSpec((d, tn), lambda i, j: (0, nj + j)),
                  pl.BlockSpec((1, tn), lambda i, j: (0, j)),
                  pl.BlockSpec((1, tn), lambda i, j: (0, nj + j)),
                  pl.BlockSpec((tm, wa), lambda i, j: (i, 0)),
                  pl.BlockSpec((wa, tn), lambda i, j: (0, j)),
                  pl.BlockSpec((tm, wb), lambda i, j: (i, 0)),
                  pl.BlockSpec((wb, tn), lambda i, j: (0, j))],
        out_specs=pl.BlockSpec((tm, tn), lambda i, j: (i, j)),
        out_shape=jax.ShapeDtypeStruct((m, d), BF16),
        compiler_params=_params("parallel", "parallel"),
        name="gate_merge",
    )(h, w_gate, w_gate, b_gate, b_gate, oa, pa, ob, pb)


def _peer_scores_kernel(x_ref, w_ref, sk_ref, o_ref):
    q = jnp.dot(x_ref[...], w_ref[...], preferred_element_type=F32).astype(BF16)
    for p in range(2):
        o_ref[p] = lax.dot_general(sk_ref[p], q[:, p * LANE:(p + 1) * LANE], NT_DIMS,
                                   preferred_element_type=F32)


def _peer_scores(x, wq, sk, tm=1024):
    t, d = x.shape
    ph = sk.shape[0] // 2
    half = sk.shape[2]
    tm = _tile(t, tm)
    return pl.pallas_call(
        _peer_scores_kernel,
        grid=(t // tm, ph),
        in_specs=[pl.BlockSpec((tm, d), lambda i, h: (i, 0)),
                  pl.BlockSpec((d, 2 * half), lambda i, h: (0, h)),
                  pl.BlockSpec((2, N_KEYS, half), lambda i, h: (h, 0, 0))],
        out_specs=pl.BlockSpec((2, N_KEYS, tm), lambda i, h: (h, 0, i)),
        out_shape=jax.ShapeDtypeStruct((2 * ph, N_KEYS, t), F32),
        compiler_params=_params("parallel", "parallel"),
        name="peer_scores",
    )(x, wq, sk)


def _top_values(vals, n):
    rowid = lax.broadcasted_iota(jnp.int32, vals.shape, 0)
    big = jnp.int32(vals.shape[0])
    out = []
    for _ in range(n):
        m = jnp.max(vals, axis=0, keepdims=True)
        first = jnp.min(jnp.where(vals == m, rowid, big), axis=0, keepdims=True)
        vals = jnp.where(rowid == first, -jnp.inf, vals)
        out.append(m)
    return out


def _peer_topk_kernel(s_ref, e1_ref, e2_ref, tau_ref):
    s1, s2 = s_ref[0], s_ref[1]
    v1 = _top_values(s1, PEER_TOPK)
    v2 = _top_values(s2, PEER_TOPK)
    v2s = jnp.concatenate(v2, axis=0)
    cand = jnp.concatenate([a + v2s for a in v1], axis=0)
    c = _top_values(cand, PEER_TOPK)
    z = c[0] * 0.0
    for ck in c:
        z = z + jnp.exp(ck - c[0])
    tau_ref[0] = c[PEER_TOPK - 1]
    e1_ref[0] = jnp.exp(s1 - v1[0])
    e2_ref[0] = jnp.exp(s2 - v2[0]) / z


def _peer_topk(st, tt=256):
    g, n, t = st.shape
    ph = g // 2
    tt = _tile(t, tt)
    tab = jax.ShapeDtypeStruct((ph, n, t), F32)
    return pl.pallas_call(
        _peer_topk_kernel,
        grid=(ph, t // tt),
        in_specs=[pl.BlockSpec((2, n, tt), lambda h, i: (h, 0, i))],
        out_specs=[pl.BlockSpec((1, n, tt), lambda h, i: (h, 0, i)),
                   pl.BlockSpec((1, n, tt), lambda h, i: (h, 0, i)),
                   pl.BlockSpec((1, 1, tt), lambda h, i: (h, 0, i))],
        out_shape=[tab, tab, jax.ShapeDtypeStruct((ph, 1, t), F32)],
        compiler_params=_params("parallel", "parallel"),
        name="peer_topk",
    )(st)


def _peer_main_kernel(x_ref, u_ref, vt_ref, s1_ref, s2_ref, e1_ref, e2_ref, tau_ref, o_ref, *, te, ph):
    e = pl.program_id(1)
    ht = lax.dot_general(u_ref[...], x_ref[...], NT_DIMS, preferred_element_type=F32)
    act = 0.5 * ht * (1.0 + lax.erf(ht * (2.0 ** -0.5)))
    parts = []
    for c in range(te // N_KEYS):
        i = e * (te // N_KEYS) + c
        g = None
        for h in range(ph):
            sel = (s1_ref[h, pl.ds(i, 1), :] + s2_ref[h]) >= tau_ref[h]
            w = jnp.where(sel, e1_ref[h, pl.ds(i, 1), :] * e2_ref[h], 0.0)
            g = w if g is None else g + w
        parts.append((g * act[c * N_KEYS:(c + 1) * N_KEYS]).astype(BF16))
    at = jnp.concatenate(parts, axis=0)
    contrib = jnp.dot(vt_ref[...], at, preferred_element_type=F32)

    @pl.when(e == 0)
    def _first():
        o_ref[...] = contrib

    @pl.when(e > 0)
    def _rest():
        o_ref[...] += contrib


def _peer_main(x, u, vt, st4, e1, e2, tau, tm=512, te=256):
    t, d = x.shape
    n_exp = u.shape[0]
    ph = e1.shape[0]
    tm, te = _tile(t, tm), _tile(n_exp, te)
    return pl.pallas_call(
        functools.partial(_peer_main_kernel, te=te, ph=ph),
        grid=(t // tm, n_exp // te),
        in_specs=[pl.BlockSpec((tm, d), lambda i, e: (i, 0)),
                  pl.BlockSpec((te, d), lambda i, e: (e, 0)),
                  pl.BlockSpec((d, te), lambda i, e: (0, e)),
                  pl.BlockSpec((ph, None, N_KEYS, tm), lambda i, e: (0, 0, 0, i)),
                  pl.BlockSpec((ph, None, N_KEYS, tm), lambda i, e: (0, 1, 0, i)),
                  pl.BlockSpec((ph, N_KEYS, tm), lambda i, e: (0, 0, i)),
                  pl.BlockSpec((ph, N_KEYS, tm), lambda i, e: (0, 0, i)),
                  pl.BlockSpec((ph, 1, tm), lambda i, e: (0, 0, i))],
        out_specs=pl.BlockSpec((d, tm), lambda i, e: (0, i)),
        out_shape=jax.ShapeDtypeStruct((d, t), F32),
        compiler_params=_params("parallel", "arbitrary"),
        name="peer_experts",
    )(x, u, vt, st4, st4, e1, e2, tau)


def _final_kernel(x_ref, pt_ref, g_ref, o_ref):
    o_ref[...] = _rms(x_ref[...] + pt_ref[...].T, g_ref[...])


def _final(x1, pt, g, tm=256):
    t, d = x1.shape
    tm = _tile(t, tm)
    return pl.pallas_call(
        _final_kernel,
        grid=(t // tm,),
        in_specs=[pl.BlockSpec((tm, d), lambda i: (i, 0)),
                  pl.BlockSpec((d, tm), lambda i: (0, i)),
                  pl.BlockSpec((1, d), lambda i: (0, 0))],
        out_specs=pl.BlockSpec((tm, d), lambda i: (i, 0)),
        out_shape=jax.ShapeDtypeStruct((t, d), F32),
        compiler_params=_params("parallel"),
        name="final_norm",
    )(x1, pt, g.reshape(1, d))


def _rot_cols(w):
    q = ROPE // 4
    idx = np.concatenate([np.arange(q, 2 * q), np.arange(0, q), np.arange(3 * q, 4 * q), np.arange(2 * q, 3 * q)])
    sign = np.concatenate([-np.ones(q), np.ones(q), -np.ones(q), np.ones(q)]).astype(np.float32)
    return w[..., idx] * sign


def _rope_tables(seq):
    t = jnp.arange(seq)
    half = ROPE // 4
    freqs = ROPE_BASE ** (-jnp.arange(half, dtype=F32) / half)
    ang_r = (t // GRID_W).astype(F32)[:, None] * freqs[None, :]
    ang_c = (t % GRID_W).astype(F32)[:, None] * freqs[None, :]
    zeros = jnp.zeros((seq, LANE - ROPE), F32)
    cos = jnp.concatenate([jnp.cos(ang_r), jnp.cos(ang_r), jnp.cos(ang_c), jnp.cos(ang_c), zeros], axis=1)
    sin = jnp.concatenate([jnp.sin(ang_r), jnp.sin(ang_r), jnp.sin(ang_c), jnp.sin(ang_c), zeros], axis=1)
    return cos, sin


def _na_toeplitz(rpb):
    cols = np.arange(GRID_W)
    start = np.clip(cols - NA_KW // 2, 0, GRID_W - NA_KW)
    ck = cols[None, :]
    inside = (ck >= start[:, None]) & (ck < start[:, None] + NA_KW)
    dc = np.clip(ck - cols[:, None] + NA_KW - 1, 0, 2 * NA_KW - 2)
    return jnp.where(inside[None, None], rpb[:, :, dc], -jnp.inf)


def kernel(x, attn_norm, w_in, q_norm, w_uq, kv_norm, w_ukv, w_branch_mla, na_rpb, w_branch_na,
           w_gate, b_gate, w_out, ffn_norm, peer_w_query, peer_subkeys, peer_u, peer_v, final_norm):
    batch, seq, d = x.shape
    t = batch * seq
    depth = w_in.shape[0]
    q_rank, kv_rank = q_norm.shape[1], kv_norm.shape[1]
    mla_heads = w_uq.shape[2] // MLA_QK
    na_heads = na_rpb.shape[1]
    na_w = na_heads * NA_D
    cos, sin = _rope_tables(seq)
    xf = x.reshape(t, d)
    for l in range(depth):
        o1, o2, o3 = q_rank, q_rank + kv_rank, q_rank + kv_rank + ROPE
        w_na = w_in[l][:, o3:].astype(BF16)
        cq_w = -(-q_rank // kv_rank) * kv_rank
        pad = jnp.zeros((d, cq_w - q_rank), F32)
        w_kr = w_in[l][:, o2:o3]
        w_lat = jnp.concatenate([w_in[l][:, :o1], pad, w_in[l][:, o1:o2], w_kr, _rot_cols(w_kr)], axis=1).astype(BF16)
        ckv_blk = cq_w // kv_rank
        kr_blk = (cq_w + kv_rank) // LANE

        wq3 = w_uq[l].reshape(q_rank, mla_heads, MLA_QK)
        wq = jnp.concatenate([wq3[..., :NOPE], wq3[..., NOPE:], _rot_cols(wq3[..., NOPE:])], axis=-1)
        wq = wq.reshape(q_rank, mla_heads * QK_PAD).astype(BF16)
        wkv3 = w_ukv[l].reshape(kv_rank, mla_heads, NOPE + MLA_V)
        wk = wkv3[..., :NOPE].reshape(kv_rank, mla_heads * NOPE).astype(BF16)
        wv = wkv3[..., NOPE:].reshape(kv_rank, mla_heads * MLA_V).astype(BF16)

        h = _rmsnorm_cast(xf, attn_norm[l])
        qkv = _matmul(h, w_na, BF16, 1024, 1024, name="na_proj")
        lat = _matmul(h, w_lat, F32, 512, w_lat.shape[1], name="mla_latent_proj")
        q_cat = _mla_q(lat, q_norm[l], wq, cos, sin, mla_heads, seq)
        k_cat, v_mla = _mla_kv(lat, ckv_blk, kr_blk, kv_norm[l], wk, wv, cos, sin, mla_heads, seq)
        o_a = _mla_attn(q_cat, k_cat, v_mla, batch, seq, mla_heads)
        o_b = _na_attn(qkv, _na_toeplitz(na_rpb[l]), batch, seq, na_heads)
        merged = _gate_merge(h, w_gate[l].astype(BF16), b_gate[l].reshape(1, -1), o_a,
                             w_branch_mla[l].astype(BF16), o_b, w_branch_na[l].astype(BF16))
        xf = _matmul(merged, w_out[l].astype(BF16), F32, 512, 1024, residual=xf, name="out_proj")

        h2 = _rmsnorm_cast(xf, ffn_norm[l])
        ph = peer_subkeys.shape[1]
        sk = peer_subkeys[l].reshape(2 * ph, N_KEYS, -1).astype(BF16)
        st = _peer_scores(h2, peer_w_query[l].astype(BF16), sk)
        e1, e2, tau = _peer_topk(st)
        peer_t = _peer_main(h2, peer_u[l].astype(BF16), peer_v[l].T.astype(BF16),
                            st.reshape(ph, 2, N_KEYS, t), e1, e2, tau)
        if l + 1 < depth:
            xf = xf + peer_t.T
    return _final(xf, peer_t, final_norm).reshape(batch, seq, d)
```

```python
import functools
import math

import jax
import jax.numpy as jnp
import numpy as np
from jax import lax
from jax.experimental import pallas as pl
from jax.experimental.pallas import tpu as pltpu

F32 = jnp.float32
BF16 = jnp.bfloat16

EPS = 1e-6
ROPE_BASE = 10000.0
GRID_W = 64
NOPE = 128
ROPE = 64
MLA_V = 128
MLA_QK = NOPE + ROPE
QK_PAD = 256
NA_D = 128
NA_KH = 8
NA_KW = 16
NA_ROWS_PER_BLOCK = 8
NA_KEY_ROWS = 16
N_KEYS = 128
PEER_TOPK = 16
LANE = 128
PEER_OUT_ROWS = 512
PEER_IN_ROWS = 256
VMEM_LIMIT = 56 * 1024 * 1024

NT_DIMS = (((1,), (1,)), ((), ()))


def _params(*sem):
    return pltpu.CompilerParams(dimension_semantics=sem, vmem_limit_bytes=VMEM_LIMIT)


def _tile(n, pref):
    t = min(n, pref)
    while n % t:
        t //= 2
    return t


def _rms(x, g):
    return x * lax.rsqrt(jnp.mean(x * x, axis=-1, keepdims=True) + EPS) * g


def _rmsnorm_kernel(x_ref, g_ref, o_ref, *ot_ref):
    y = _rms(x_ref[...], g_ref[...])
    o_ref[...] = y.astype(o_ref.dtype)
    for r in ot_ref:
        r[...] = y.T.astype(r.dtype)


def _rmsnorm_cast(x, g, tm=256, transposed=False):
    m, d = x.shape
    tm = _tile(m, tm)
    out_specs = [pl.BlockSpec((tm, d), lambda i: (i, 0))]
    out_shape = [jax.ShapeDtypeStruct((m, d), BF16)]
    if transposed:
        out_specs.append(pl.BlockSpec((d, tm), lambda i: (0, i)))
        out_shape.append(jax.ShapeDtypeStruct((d, m), BF16))
    out = pl.pallas_call(
        _rmsnorm_kernel,
        grid=(m // tm,),
        in_specs=[pl.BlockSpec((tm, d), lambda i: (i, 0)),
                  pl.BlockSpec((1, d), lambda i: (0, 0))],
        out_specs=out_specs,
        out_shape=out_shape,
        compiler_params=_params("parallel"),
        name="rmsnorm_cast",
    )(x, g.reshape(1, d))
    return out if transposed else out[0]


def _mm_kernel(a_ref, w_ref, o_ref):
    o_ref[...] = jnp.dot(a_ref[...], w_ref[...], preferred_element_type=F32).astype(o_ref.dtype)


def _mm_res_kernel(a_ref, w_ref, r_ref, o_ref):
    acc = jnp.dot(a_ref[...], w_ref[...], preferred_element_type=F32)
    o_ref[...] = (r_ref[...] + acc).astype(o_ref.dtype)


def _matmul(a, w, out_dtype, tm, tn, residual=None, name="matmul"):
    m, k = a.shape
    n = w.shape[1]
    tm, tn = _tile(m, tm), _tile(n, tn)
    in_specs = [pl.BlockSpec((tm, k), lambda i, j: (i, 0)),
                pl.BlockSpec((k, tn), lambda i, j: (0, j))]
    args = [a, w]
    body = _mm_kernel
    if residual is not None:
        in_specs.append(pl.BlockSpec((tm, tn), lambda i, j: (i, j)))
        args.append(residual)
        body = _mm_res_kernel
    return pl.pallas_call(
        body,
        grid=(m // tm, n // tn),
        in_specs=in_specs,
        out_specs=pl.BlockSpec((tm, tn), lambda i, j: (i, j)),
        out_shape=jax.ShapeDtypeStruct((m, n), out_dtype),
        compiler_params=_params("parallel", "parallel"),
        name=name,
    )(*args)


def _rope_pair(blk, cos, sin):
    return blk * cos + pltpu.roll(blk, ROPE, 1) * sin


def _mla_q_kernel(c_ref, g_ref, w_ref, cos_ref, sin_ref, o_ref, *, heads, scale):
    cn = _rms(c_ref[...], g_ref[...]).astype(BF16)
    cos, sin = cos_ref[...], sin_ref[...]
    for h in range(heads):
        y = jnp.dot(cn, w_ref[:, h * QK_PAD:(h + 1) * QK_PAD], preferred_element_type=F32) * scale
        o_ref[:, h * QK_PAD:h * QK_PAD + NOPE] = y[:, :NOPE].astype(o_ref.dtype)
        o_ref[:, h * QK_PAD + NOPE:(h + 1) * QK_PAD] = _rope_pair(y[:, NOPE:], cos, sin).astype(o_ref.dtype)


def _mla_q(lat, g, wq, cos, sin, heads, seq, tm=512):
    t = lat.shape[0]
    r = g.shape[0]
    tm = _tile(seq, tm)
    sb = seq // tm
    return pl.pallas_call(
        functools.partial(_mla_q_kernel, heads=heads, scale=MLA_QK ** -0.5),
        grid=(t // tm,),
        in_specs=[pl.BlockSpec((tm, r), lambda i: (i, 0)),
                  pl.BlockSpec((1, r), lambda i: (0, 0)),
                  pl.BlockSpec((r, heads * QK_PAD), lambda i: (0, 0)),
                  pl.BlockSpec((tm, LANE), lambda i: (i % sb, 0)),
                  pl.BlockSpec((tm, LANE), lambda i: (i % sb, 0))],
        out_specs=pl.BlockSpec((tm, heads * QK_PAD), lambda i: (i, 0)),
        out_shape=jax.ShapeDtypeStruct((t, heads * QK_PAD), BF16),
        compiler_params=_params("parallel"),
        name="mla_q_proj",
    )(lat, g.reshape(1, r), wq, cos, sin)


def _mla_kv_kernel(c_ref, kr_ref, g_ref, wk_ref, wv_ref, cos_ref, sin_ref, k_ref, v_ref, *, heads):
    cn = _rms(c_ref[...], g_ref[...]).astype(BF16)
    kpe = _rope_pair(kr_ref[...], cos_ref[...], sin_ref[...]).astype(k_ref.dtype)
    kn = jnp.dot(cn, wk_ref[...], preferred_element_type=F32).astype(k_ref.dtype)
    v_ref[...] = jnp.dot(cn, wv_ref[...], preferred_element_type=F32).astype(v_ref.dtype)
    for h in range(heads):
        k_ref[:, h * QK_PAD:h * QK_PAD + NOPE] = kn[:, h * NOPE:(h + 1) * NOPE]
        k_ref[:, h * QK_PAD + NOPE:(h + 1) * QK_PAD] = kpe


def _mla_kv(lat, ckv_blk, kr_blk, g, wk, wv, cos, sin, heads, seq, tm=512):
    t = lat.shape[0]
    r = g.shape[0]
    tm = _tile(seq, tm)
    sb = seq // tm
    return pl.pallas_call(
        functools.partial(_mla_kv_kernel, heads=heads),
        grid=(t // tm,),
        in_specs=[pl.BlockSpec((tm, r), lambda i: (i, ckv_blk)),
                  pl.BlockSpec((tm, LANE), lambda i: (i, kr_blk)),
                  pl.BlockSpec((1, r), lambda i: (0, 0)),
                  pl.BlockSpec((r, heads * NOPE), lambda i: (0, 0)),
                  pl.BlockSpec((r, heads * MLA_V), lambda i: (0, 0)),
                  pl.BlockSpec((tm, LANE), lambda i: (i % sb, 0)),
                  pl.BlockSpec((tm, LANE), lambda i: (i % sb, 0))],
        out_specs=[pl.BlockSpec((tm, heads * QK_PAD), lambda i: (i, 0)),
                   pl.BlockSpec((tm, heads * MLA_V), lambda i: (i, 0))],
        out_shape=[jax.ShapeDtypeStruct((t, heads * QK_PAD), BF16),
                   jax.ShapeDtypeStruct((t, heads * MLA_V), BF16)],
        compiler_params=_params("parallel"),
        name="mla_kv_proj",
    )(lat, lat, g.reshape(1, r), wk, wv, cos, sin)


def _mla_attn_kernel(q_ref, k_ref, v_ref, o_ref, *, tk):
    q = q_ref[...]
    m = l = acc = None
    for j in range(k_ref.shape[0] // tk):
        s = lax.dot_general(q, k_ref[j * tk:(j + 1) * tk, :], NT_DIMS, preferred_element_type=F32)
        m_blk = jnp.max(s, axis=1, keepdims=True)
        m_new = m_blk if j == 0 else jnp.maximum(m, m_blk)
        p = jnp.exp(s - m_new)
        pv = jnp.dot(p.astype(BF16), v_ref[j * tk:(j + 1) * tk, :], preferred_element_type=F32)
        if j == 0:
            l, acc = jnp.sum(p, axis=1, keepdims=True), pv
        else:
            alpha = jnp.exp(m - m_new)
            l = alpha * l + jnp.sum(p, axis=1, keepdims=True)
            acc = alpha * acc + pv
        m = m_new
    o_ref[...] = (acc / l).astype(o_ref.dtype)


def _mla_attn(q, k, v, batch, seq, heads, tq=512, tk=1024):
    t = q.shape[0]
    tq, tk = _tile(seq, tq), _tile(seq, tk)
    qb = seq // tq
    return pl.pallas_call(
        functools.partial(_mla_attn_kernel, tk=tk),
        grid=(batch, heads, qb),
        in_specs=[pl.BlockSpec((tq, QK_PAD), lambda b, h, i: (b * qb + i, h)),
                  pl.BlockSpec((seq, QK_PAD), lambda b, h, i: (b, h)),
                  pl.BlockSpec((seq, MLA_V), lambda b, h, i: (b, h))],
        out_specs=pl.BlockSpec((tq, MLA_V), lambda b, h, i: (b * qb + i, h)),
        out_shape=jax.ShapeDtypeStruct((t, heads * MLA_V), BF16),
        compiler_params=_params("parallel", "parallel", "parallel"),
        name="mla_attention",
    )(q, k, v)


def _na_block_layout(rows):
    nb = rows // NA_ROWS_PER_BLOCK
    kh = min(NA_KH, rows)
    classes = []
    for blk in (0, 1, nb - 1):
        r0 = blk * NA_ROWS_PER_BLOCK
        ks = min(max(r0 - NA_KH // 2, 0), rows - NA_KEY_ROWS)
        entries = []
        for rq in range(NA_ROWS_PER_BLOCK):
            r = r0 + rq
            rs = min(max(r - kh // 2, 0), rows - kh)
            for j in range(kh):
                entries.append((rq, rs + j - ks, rs + j - r + NA_KH - 1))
        classes.append(entries)
    return classes


def _na_attn_kernel(q_ref, k_ref, v_ref, toep_ref, o_ref, bias_ref, *, rows, scale):
    b, blk = pl.program_id(1), pl.program_id(2)
    nb = rows // NA_ROWS_PER_BLOCK

    @pl.when((b == 0) & (blk == 0))
    def _build_bias():
        bias_ref[...] = jnp.full(bias_ref.shape, -jnp.inf, F32)
        for c, entries in enumerate(_na_block_layout(rows)):
            for rq, rk, d in entries:
                bias_ref[c, rq * GRID_W:(rq + 1) * GRID_W, rk * GRID_W:(rk + 1) * GRID_W] = toep_ref[0, d]

    r0 = blk * NA_ROWS_PER_BLOCK
    ks = jnp.clip(r0 - NA_KH // 2, 0, rows - NA_KEY_ROWS)
    off = pl.multiple_of(ks * GRID_W, GRID_W)
    nk = NA_KEY_ROWS * GRID_W
    cls = jnp.where(blk == 0, 0, jnp.where(blk == nb - 1, 2, 1))
    s = lax.dot_general(q_ref[...], k_ref[pl.ds(off, nk), :], NT_DIMS, preferred_element_type=F32)
    s = s * scale + bias_ref[cls]
    m = jnp.max(s, axis=1, keepdims=True)
    p = jnp.exp(s - m)
    l = jnp.sum(p, axis=1, keepdims=True)
    o = jnp.dot(p.astype(BF16), v_ref[pl.ds(off, nk), :], preferred_element_type=F32)
    o_ref[...] = (o / l).astype(o_ref.dtype)


def _na_attn(qkv, toep, batch, seq, heads):
    t = qkv.shape[0]
    rows = seq // GRID_W
    nb = rows // NA_ROWS_PER_BLOCK
    tq = NA_ROWS_PER_BLOCK * GRID_W
    return pl.pallas_call(
        functools.partial(_na_attn_kernel, rows=rows, scale=NA_D ** -0.5),
        grid=(heads, batch, nb),
        in_specs=[pl.BlockSpec((tq, NA_D), lambda h, b, i: (b * nb + i, h)),
                  pl.BlockSpec((seq, NA_D), lambda h, b, i: (b, heads + h)),
                  pl.BlockSpec((seq, NA_D), lambda h, b, i: (b, 2 * heads + h)),
                  pl.BlockSpec((1, 2 * NA_KH - 1, GRID_W, GRID_W), lambda h, b, i: (h, 0, 0, 0))],
        out_specs=pl.BlockSpec((tq, NA_D), lambda h, b, i: (b * nb + i, h)),
        out_shape=jax.ShapeDtypeStruct((t, heads * NA_D), BF16),
        scratch_shapes=[pltpu.VMEM((3, tq, NA_KEY_ROWS * GRID_W), F32)],
        compiler_params=_params("arbitrary", "arbitrary", "arbitrary"),
        name="na_attention",
    )(qkv, qkv, qkv, toep)


def _gate_merge_kernel(h_ref, wga_ref, wgb_ref, ba_ref, bb_ref, oa_ref, pa_ref, ob_ref, pb_ref, o_ref):
    h = h_ref[...]
    ga = jax.nn.sigmoid(jnp.dot(h, wga_ref[...], preferred_element_type=F32) + ba_ref[...])
    ya = jnp.dot(oa_ref[...], pa_ref[...], preferred_element_type=F32)
    acc = ga * ya
    gb = jax.nn.sigmoid(jnp.dot(h, wgb_ref[...], preferred_element_type=F32) + bb_ref[...])
    yb = jnp.dot(ob_ref[...], pb_ref[...], preferred_element_type=F32)
    o_ref[...] = (acc + gb * yb).astype(o_ref.dtype)


def _gate_merge(h, w_gate, b_gate, oa, pa, ob, pb, tm=512, tn=512):
    m, d = h.shape
    wa, wb = oa.shape[1], ob.shape[1]
    tm, tn = _tile(m, tm), _tile(d, tn)
    nj = d // tn
    return pl.pallas_call(
        _gate_merge_kernel,
        grid=(m // tm, nj),
        in_specs=[pl.BlockSpec((tm, d), lambda i, j: (i, 0)),
                  pl.BlockSpec((d, tn), lambda i, j: (0, j)),
                  pl.BlockSpec((d, tn), lambda i, j: (0, nj + j)),
                  pl.BlockSpec((1, tn), lambda i, j: (0, j)),
                  pl.BlockSpec((1, tn), lambda i, j: (0, nj + j)),
                  pl.BlockSpec((tm, wa), lambda i, j: (i, 0)),
                  pl.BlockSpec((wa, tn), lambda i, j: (0, j)),
                  pl.BlockSpec((tm, wb), lambda i, j: (i, 0)),
                  pl.BlockSpec((wb, tn), lambda i, j: (0, j))],
        out_specs=pl.BlockSpec((tm, tn), lambda i, j: (i, j)),
        out_shape=jax.ShapeDtypeStruct((m, d), BF16),
        compiler_params=_params("parallel", "parallel"),
        name="gate_merge",
    )(h, w_gate, w_gate, b_gate, b_gate, oa, pa, ob, pb)


def _peer_scores_kernel(x_ref, w_ref, sk_ref, o_ref):
    q = jnp.dot(x_ref[...], w_ref[...], preferred_element_type=F32).astype(BF16)
    for p in range(2):
        o_ref[p] = lax.dot_general(sk_ref[p], q[:, p * LANE:(p + 1) * LANE], NT_DIMS,
                                   preferred_element_type=F32)


def _peer_scores(x, wq, sk, tm=1024):
    t, d = x.shape
    ph = sk.shape[0] // 2
    half = sk.shape[2]
    tm = _tile(t, tm)
    return pl.pallas_call(
        _peer_scores_kernel,
        grid=(t // tm, ph),
        in_specs=[pl.BlockSpec((tm, d), lambda i, h: (i, 0)),
                  pl.BlockSpec((d, 2 * half), lambda i, h: (0, h)),
                  pl.BlockSpec((2, N_KEYS, half), lambda i, h: (h, 0, 0))],
        out_specs=pl.BlockSpec((2, N_KEYS, tm), lambda i, h: (h, 0, i)),
        out_shape=jax.ShapeDtypeStruct((2 * ph, N_KEYS, t), F32),
        compiler_params=_params("parallel", "parallel"),
        name="peer_scores",
    )(x, wq, sk)


def _top_values(vals, n):
    rowid = lax.broadcasted_iota(jnp.int32, vals.shape, 0)
    big = jnp.int32(vals.shape[0])
    out = []
    for _ in range(n):
        m = jnp.max(vals, axis=0, keepdims=True)
        first = jnp.min(jnp.where(vals == m, rowid, big), axis=0, keepdims=True)
        vals = jnp.where(rowid == first, -jnp.inf, vals)
        out.append(m)
    return out


def _peer_topk_kernel(s_ref, e1_ref, e2_ref, tau_ref):
    s1, s2 = s_ref[0], s_ref[1]
    v1 = _top_values(s1, PEER_TOPK)
    v2 = _top_values(s2, PEER_TOPK)
    v2s = jnp.concatenate(v2, axis=0)
    cand = jnp.concatenate([a + v2s for a in v1], axis=0)
    c = _top_values(cand, PEER_TOPK)
    z = c[0] * 0.0
    for ck in c:
        z = z + jnp.exp(ck - c[0])
    tau_ref[0] = c[PEER_TOPK - 1]
    e1_ref[0] = jnp.exp(s1 - v1[0])
    e2_ref[0] = jnp.exp(s2 - v2[0]) / z


def _peer_topk(st, tt=256):
    g, n, t = st.shape
    ph = g // 2
    tt = _tile(t, tt)
    tab = jax.ShapeDtypeStruct((ph, n, t), F32)
    return pl.pallas_call(
        _peer_topk_kernel,
        grid=(ph, t // tt),
        in_specs=[pl.BlockSpec((2, n, tt), lambda h, i: (h, 0, i))],
        out_specs=[pl.BlockSpec((1, n, tt), lambda h, i: (h, 0, i)),
                   pl.BlockSpec((1, n, tt), lambda h, i: (h, 0, i)),
                   pl.BlockSpec((1, 1, tt), lambda h, i: (h, 0, i))],
        out_shape=[tab, tab, jax.ShapeDtypeStruct((ph, 1, t), F32)],
        compiler_params=_params("parallel", "parallel"),
        name="peer_topk",
    )(st)


def _peer_main_kernel(xt_ref, u_ref, vt_ref, s1_ref, s2_ref, e1_ref, e2_ref, tau_ref, o_ref, g_ref, at_ref,
                      *, te, ph, n_e):
    e = pl.program_id(1)
    d = o_ref.shape[0]
    groups = te // N_KEYS
    tile = jnp.minimum(e, n_e - 1)

    @pl.when(e == 0)
    def _init():
        o_ref[...] = jnp.zeros(o_ref.shape, o_ref.dtype)
        at_ref[...] = jnp.zeros(at_ref.shape, at_ref.dtype)

    def gates(r0, r1, zero):
        i = tile * groups + r0 // N_KEYS
        j0, j1 = r0 % N_KEYS, r0 % N_KEYS + (r1 - r0)
        g = None
        for h in range(ph):
            sel = (s1_ref[h, pl.ds(i, 1), :] + s2_ref[h, j0:j1, :]) >= tau_ref[h] + zero
            w = jnp.where(sel, e1_ref[h, pl.ds(i, 1), :] * e2_ref[h, j0:j1, :], 0.0)
            g = w if g is None else g + w
        g_ref[r0:r1, :] = g

    m_rows = min(d, PEER_OUT_ROWS)
    n_m = d // m_rows
    g_rows = min(te // n_m, N_KEYS)
    at_prev = at_ref[...]
    for m in range(n_m):
        rows = slice(m * m_rows, (m + 1) * m_rows)
        o_ref[rows, :] += jnp.dot(vt_ref[rows, :], at_prev, preferred_element_type=F32)
        bits = lax.bitcast_convert_type(o_ref[m * m_rows:m * m_rows + 1, :], jnp.int32)
        zero = lax.shift_right_logical(lax.shift_right_logical(bits, 31), 1).astype(F32)
        for r0 in range(m * (te // n_m), (m + 1) * (te // n_m), g_rows):
            gates(r0, r0 + g_rows, zero)

    u_rows = min(te, PEER_IN_ROWS)
    for r in range(te // u_rows):
        rows = slice(r * u_rows, (r + 1) * u_rows)
        ht = jnp.dot(u_ref[rows, :], xt_ref[...], preferred_element_type=F32)
        act = 0.5 * ht * (1.0 + lax.erf(ht * (2.0 ** -0.5)))
        at_ref[rows, :] = (g_ref[rows, :] * act).astype(at_ref.dtype)


def _peer_main(xt, u, vt, st4, e1, e2, tau, tm=512, te=512):
    d, t = xt.shape
    n_exp = u.shape[0]
    ph = e1.shape[0]
    tm, te = _tile(t, tm), _tile(n_exp, te)
    n_e = n_exp // te
    once = pl.Buffered(1)
    return pl.pallas_call(
        functools.partial(_peer_main_kernel, te=te, ph=ph, n_e=n_e),
        grid=(t // tm, n_e + 1),
        in_specs=[pl.BlockSpec((d, tm), lambda i, e: (0, i), pipeline_mode=once),
                  pl.BlockSpec((te, d), lambda i, e: (jnp.minimum(e, n_e - 1), 0)),
                  pl.BlockSpec((d, te), lambda i, e: (0, jnp.maximum(e - 1, 0))),
                  pl.BlockSpec((ph, None, N_KEYS, tm), lambda i, e: (0, 0, 0, i), pipeline_mode=once),
                  pl.BlockSpec((ph, None, N_KEYS, tm), lambda i, e: (0, 1, 0, i), pipeline_mode=once),
                  pl.BlockSpec((ph, N_KEYS, tm), lambda i, e: (0, 0, i), pipeline_mode=once),
                  pl.BlockSpec((ph, N_KEYS, tm), lambda i, e: (0, 0, i), pipeline_mode=once),
                  pl.BlockSpec((ph, 1, tm), lambda i, e: (0, 0, i), pipeline_mode=once)],
        out_specs=pl.BlockSpec((d, tm), lambda i, e: (0, i)),
        out_shape=jax.ShapeDtypeStruct((d, t), F32),
        scratch_shapes=[pltpu.VMEM((te, tm), F32), pltpu.VMEM((te, tm), BF16)],
        compiler_params=_params("parallel", "arbitrary"),
        name="peer_experts",
    )(xt, u, vt, st4, st4, e1, e2, tau)


def _final_kernel(x_ref, pt_ref, g_ref, o_ref):
    o_ref[...] = _rms(x_ref[...] + pt_ref[...].T, g_ref[...])


def _final(x1, pt, g, tm=256):
    t, d = x1.shape
    tm = _tile(t, tm)
    return pl.pallas_call(
        _final_kernel,
        grid=(t // tm,),
        in_specs=[pl.BlockSpec((tm, d), lambda i: (i, 0)),
                  pl.BlockSpec((d, tm), lambda i: (0, i)),
                  pl.BlockSpec((1, d), lambda i: (0, 0))],
        out_specs=pl.BlockSpec((tm, d), lambda i: (i, 0)),
        out_shape=jax.ShapeDtypeStruct((t, d), F32),
        compiler_params=_params("parallel"),
        name="final_norm",
    )(x1, pt, g.reshape(1, d))


def _rot_cols(w):
    q = ROPE // 4
    idx = np.concatenate([np.arange(q, 2 * q), np.arange(0, q), np.arange(3 * q, 4 * q), np.arange(2 * q, 3 * q)])
    sign = np.concatenate([-np.ones(q), np.ones(q), -np.ones(q), np.ones(q)]).astype(np.float32)
    return w[..., idx] * sign


def _rope_tables(seq):
    t = jnp.arange(seq)
    half = ROPE // 4
    freqs = ROPE_BASE ** (-jnp.arange(half, dtype=F32) / half)
    ang_r = (t // GRID_W).astype(F32)[:, None] * freqs[None, :]
    ang_c = (t % GRID_W).astype(F32)[:, None] * freqs[None, :]
    zeros = jnp.zeros((seq, LANE - ROPE), F32)
    cos = jnp.concatenate([jnp.cos(ang_r), jnp.cos(ang_r), jnp.cos(ang_c), jnp.cos(ang_c), zeros], axis=1)
    sin = jnp.concatenate([jnp.sin(ang_r), jnp.sin(ang_r), jnp.sin(ang_c), jnp.sin(ang_c), zeros], axis=1)
    return cos, sin


def _na_toeplitz(rpb):
    cols = np.arange(GRID_W)
    start = np.clip(cols - NA_KW // 2, 0, GRID_W - NA_KW)
    ck = cols[None, :]
    inside = (ck >= start[:, None]) & (ck < start[:, None] + NA_KW)
    dc = np.clip(ck - cols[:, None] + NA_KW - 1, 0, 2 * NA_KW - 2)
    return jnp.where(inside[None, None], rpb[:, :, dc], -jnp.inf)


def kernel(x, attn_norm, w_in, q_norm, w_uq, kv_norm, w_ukv, w_branch_mla, na_rpb, w_branch_na,
           w_gate, b_gate, w_out, ffn_norm, peer_w_query, peer_subkeys, peer_u, peer_v, final_norm):
    batch, seq, d = x.shape
    t = batch * seq
    depth = w_in.shape[0]
    q_rank, kv_rank = q_norm.shape[1], kv_norm.shape[1]
    mla_heads = w_uq.shape[2] // MLA_QK
    na_heads = na_rpb.shape[1]
    na_w = na_heads * NA_D
    cos, sin = _rope_tables(seq)
    xf = x.reshape(t, d)
    for l in range(depth):
        o1, o2, o3 = q_rank, q_rank + kv_rank, q_rank + kv_rank + ROPE
        w_na = w_in[l][:, o3:].astype(BF16)
        cq_w = -(-q_rank // kv_rank) * kv_rank
        pad = jnp.zeros((d, cq_w - q_rank), F32)
        w_kr = w_in[l][:, o2:o3]
        w_lat = jnp.concatenate([w_in[l][:, :o1], pad, w_in[l][:, o1:o2], w_kr, _rot_cols(w_kr)], axis=1).astype(BF16)
        ckv_blk = cq_w // kv_rank
        kr_blk = (cq_w + kv_rank) // LANE

        wq3 = w_uq[l].reshape(q_rank, mla_heads, MLA_QK)
        wq = jnp.concatenate([wq3[..., :NOPE], wq3[..., NOPE:], _rot_cols(wq3[..., NOPE:])], axis=-1)
        wq = wq.reshape(q_rank, mla_heads * QK_PAD).astype(BF16)
        wkv3 = w_ukv[l].reshape(kv_rank, mla_heads, NOPE + MLA_V)
        wk = wkv3[..., :NOPE].reshape(kv_rank, mla_heads * NOPE).astype(BF16)
        wv = wkv3[..., NOPE:].reshape(kv_rank, mla_heads * MLA_V).astype(BF16)

        h = _rmsnorm_cast(xf, attn_norm[l])
        qkv = _matmul(h, w_na, BF16, 1024, 1024, name="na_proj")
        lat = _matmul(h, w_lat, F32, 512, w_lat.shape[1], name="mla_latent_proj")
        q_cat = _mla_q(lat, q_norm[l], wq, cos, sin, mla_heads, seq)
        k_cat, v_mla = _mla_kv(lat, ckv_blk, kr_blk, kv_norm[l], wk, wv, cos, sin, mla_heads, seq)
        o_a = _mla_attn(q_cat, k_cat, v_mla, batch, seq, mla_heads)
        o_b = _na_attn(qkv, _na_toeplitz(na_rpb[l]), batch, seq, na_heads)
        merged = _gate_merge(h, w_gate[l].astype(BF16), b_gate[l].reshape(1, -1), o_a,
                             w_branch_mla[l].astype(BF16), o_b, w_branch_na[l].astype(BF16))
        xf = _matmul(merged, w_out[l].astype(BF16), F32, 512, 1024, residual=xf, name="out_proj")

        h2, h2t = _rmsnorm_cast(xf, ffn_norm[l], transposed=True)
        ph = peer_subkeys.shape[1]
        sk = peer_subkeys[l].reshape(2 * ph, N_KEYS, -1).astype(BF16)
        st = _peer_scores(h2, peer_w_query[l].astype(BF16), sk)
        e1, e2, tau = _peer_topk(st)
        peer_t = _peer_main(h2t, peer_u[l].astype(BF16), peer_v[l].T.astype(BF16),
                            st.reshape(ph, 2, N_KEYS, t), e1, e2, tau)
        if l + 1 < depth:
            xf = xf + peer_t.T
    return _final(xf, peer_t, final_norm).reshape(batch, seq, d)
```

```python
import functools
import math

import jax
import jax.numpy as jnp
import numpy as np
from jax import lax
from jax.experimental import pallas as pl
from jax.experimental.pallas import tpu as pltpu

F32 = jnp.float32
BF16 = jnp.bfloat16

EPS = 1e-6
ROPE_BASE = 10000.0
GRID_W = 64
NOPE = 128
ROPE = 64
MLA_V = 128
MLA_QK = NOPE + ROPE
QK_PAD = 256
NA_D = 128
NA_KH = 8
NA_KW = 16
NA_ROWS_PER_BLOCK = 4
NA_KEY_ROWS = 12
NA_BLOCKS_PER_STEP = 2
N_KEYS = 128
PEER_TOPK = 16
LANE = 128
PEER_GATE_LANES = 128
PEER_IN_ROWS = 256
VMEM_LIMIT = 56 * 1024 * 1024

NT_DIMS = (((1,), (1,)), ((), ()))


def _params(*sem):
    return pltpu.CompilerParams(dimension_semantics=sem, vmem_limit_bytes=VMEM_LIMIT)


def _tile(n, pref):
    t = min(n, pref)
    while n % t:
        t //= 2
    return t


def _rms(x, g):
    return x * lax.rsqrt(jnp.mean(x * x, axis=-1, keepdims=True) + EPS) * g


def _rmsnorm_kernel(x_ref, g_ref, o_ref, *ot_ref):
    y = _rms(x_ref[...], g_ref[...])
    o_ref[...] = y.astype(o_ref.dtype)
    for r in ot_ref:
        r[...] = y.T.astype(r.dtype)


def _rmsnorm_cast(x, g, tm=256, transposed=False):
    m, d = x.shape
    tm = _tile(m, tm)
    out_specs = [pl.BlockSpec((tm, d), lambda i: (i, 0))]
    out_shape = [jax.ShapeDtypeStruct((m, d), BF16)]
    if transposed:
        out_specs.append(pl.BlockSpec((d, tm), lambda i: (0, i)))
        out_shape.append(jax.ShapeDtypeStruct((d, m), BF16))
    out = pl.pallas_call(
        _rmsnorm_kernel,
        grid=(m // tm,),
        in_specs=[pl.BlockSpec((tm, d), lambda i: (i, 0)),
                  pl.BlockSpec((1, d), lambda i: (0, 0))],
        out_specs=out_specs,
        out_shape=out_shape,
        compiler_params=_params("parallel"),
        name="rmsnorm_cast",
    )(x, g.reshape(1, d))
    return out if transposed else out[0]


def _mm_kernel(a_ref, w_ref, o_ref):
    o_ref[...] = jnp.dot(a_ref[...], w_ref[...], preferred_element_type=F32).astype(o_ref.dtype)


def _mm_res_kernel(a_ref, w_ref, r_ref, o_ref):
    acc = jnp.dot(a_ref[...], w_ref[...], preferred_element_type=F32)
    o_ref[...] = (r_ref[...] + acc).astype(o_ref.dtype)


def _matmul(a, w, out_dtype, tm, tn, residual=None, name="matmul"):
    m, k = a.shape
    n = w.shape[1]
    tm, tn = _tile(m, tm), _tile(n, tn)
    in_specs = [pl.BlockSpec((tm, k), lambda i, j: (i, 0)),
                pl.BlockSpec((k, tn), lambda i, j: (0, j))]
    args = [a, w]
    body = _mm_kernel
    if residual is not None:
        in_specs.append(pl.BlockSpec((tm, tn), lambda i, j: (i, j)))
        args.append(residual)
        body = _mm_res_kernel
    return pl.pallas_call(
        body,
        grid=(m // tm, n // tn),
        in_specs=in_specs,
        out_specs=pl.BlockSpec((tm, tn), lambda i, j: (i, j)),
        out_shape=jax.ShapeDtypeStruct((m, n), out_dtype),
        compiler_params=_params("parallel", "parallel"),
        name=name,
    )(*args)


def _rope_pair(blk, cos, sin):
    return blk * cos + pltpu.roll(blk, ROPE, 1) * sin


def _mla_q_kernel(c_ref, g_ref, w_ref, cos_ref, sin_ref, o_ref, *, heads, scale):
    cn = _rms(c_ref[...], g_ref[...]).astype(BF16)
    cos, sin = cos_ref[...], sin_ref[...]
    for h in range(heads):
        y = jnp.dot(cn, w_ref[:, h * QK_PAD:(h + 1) * QK_PAD], preferred_element_type=F32) * scale
        o_ref[:, h * QK_PAD:h * QK_PAD + NOPE] = y[:, :NOPE].astype(o_ref.dtype)
        o_ref[:, h * QK_PAD + NOPE:(h + 1) * QK_PAD] = _rope_pair(y[:, NOPE:], cos, sin).astype(o_ref.dtype)


def _mla_q(lat, g, wq, cos, sin, heads, seq, tm=512):
    t = lat.shape[0]
    r = g.shape[0]
    tm = _tile(seq, tm)
    sb = seq // tm
    return pl.pallas_call(
        functools.partial(_mla_q_kernel, heads=heads, scale=MLA_QK ** -0.5),
        grid=(t // tm,),
        in_specs=[pl.BlockSpec((tm, r), lambda i: (i, 0)),
                  pl.BlockSpec((1, r), lambda i: (0, 0)),
                  pl.BlockSpec((r, heads * QK_PAD), lambda i: (0, 0)),
                  pl.BlockSpec((tm, LANE), lambda i: (i % sb, 0)),
                  pl.BlockSpec((tm, LANE), lambda i: (i % sb, 0))],
        out_specs=pl.BlockSpec((tm, heads * QK_PAD), lambda i: (i, 0)),
        out_shape=jax.ShapeDtypeStruct((t, heads * QK_PAD), BF16),
        compiler_params=_params("parallel"),
        name="mla_q_proj",
    )(lat, g.reshape(1, r), wq, cos, sin)


def _mla_kv_kernel(c_ref, kr_ref, g_ref, wk_ref, wv_ref, cos_ref, sin_ref, k_ref, v_ref, *, heads):
    cn = _rms(c_ref[...], g_ref[...]).astype(BF16)
    kpe = _rope_pair(kr_ref[...], cos_ref[...], sin_ref[...]).astype(k_ref.dtype)
    kn = jnp.dot(cn, wk_ref[...], preferred_element_type=F32).astype(k_ref.dtype)
    v_ref[...] = jnp.dot(cn, wv_ref[...], preferred_element_type=F32).astype(v_ref.dtype)
    for h in range(heads):
        k_ref[:, h * QK_PAD:h * QK_PAD + NOPE] = kn[:, h * NOPE:(h + 1) * NOPE]
        k_ref[:, h * QK_PAD + NOPE:(h + 1) * QK_PAD] = kpe


def _mla_kv(lat, ckv_blk, kr_blk, g, wk, wv, cos, sin, heads, seq, tm=512):
    t = lat.shape[0]
    r = g.shape[0]
    tm = _tile(seq, tm)
    sb = seq // tm
    return pl.pallas_call(
        functools.partial(_mla_kv_kernel, heads=heads),
        grid=(t // tm,),
        in_specs=[pl.BlockSpec((tm, r), lambda i: (i, ckv_blk)),
                  pl.BlockSpec((tm, LANE), lambda i: (i, kr_blk)),
                  pl.BlockSpec((1, r), lambda i: (0, 0)),
                  pl.BlockSpec((r, heads * NOPE), lambda i: (0, 0)),
                  pl.BlockSpec((r, heads * MLA_V), lambda i: (0, 0)),
                  pl.BlockSpec((tm, LANE), lambda i: (i % sb, 0)),
                  pl.BlockSpec((tm, LANE), lambda i: (i % sb, 0))],
        out_specs=[pl.BlockSpec((tm, heads * QK_PAD), lambda i: (i, 0)),
                   pl.BlockSpec((tm, heads * MLA_V), lambda i: (i, 0))],
        out_shape=[jax.ShapeDtypeStruct((t, heads * QK_PAD), BF16),
                   jax.ShapeDtypeStruct((t, heads * MLA_V), BF16)],
        compiler_params=_params("parallel"),
        name="mla_kv_proj",
    )(lat, lat, g.reshape(1, r), wk, wv, cos, sin)


def _mla_attn_kernel(q_ref, k_ref, v_ref, o_ref, *, tk):
    q = q_ref[...]
    m = l = acc = None
    for j in range(k_ref.shape[0] // tk):
        s = lax.dot_general(q, k_ref[j * tk:(j + 1) * tk, :], NT_DIMS, preferred_element_type=F32)
        m_blk = jnp.max(s, axis=1, keepdims=True)
        m_new = m_blk if j == 0 else jnp.maximum(m, m_blk)
        p = jnp.exp(s - m_new)
        pv = jnp.dot(p.astype(BF16), v_ref[j * tk:(j + 1) * tk, :], preferred_element_type=F32)
        if j == 0:
            l, acc = jnp.sum(p, axis=1, keepdims=True), pv
        else:
            alpha = jnp.exp(m - m_new)
            l = alpha * l + jnp.sum(p, axis=1, keepdims=True)
            acc = alpha * acc + pv
        m = m_new
    o_ref[...] = (acc / l).astype(o_ref.dtype)


def _mla_attn(q, k, v, batch, seq, heads, tq=1024, tk=1024):
    t = q.shape[0]
    tq, tk = _tile(seq, tq), _tile(seq, tk)
    qb = seq // tq
    return pl.pallas_call(
        functools.partial(_mla_attn_kernel, tk=tk),
        grid=(batch, heads, qb),
        in_specs=[pl.BlockSpec((tq, QK_PAD), lambda b, h, i: (b * qb + i, h)),
                  pl.BlockSpec((seq, QK_PAD), lambda b, h, i: (b, h)),
                  pl.BlockSpec((seq, MLA_V), lambda b, h, i: (b, h))],
        out_specs=pl.BlockSpec((tq, MLA_V), lambda b, h, i: (b * qb + i, h)),
        out_shape=jax.ShapeDtypeStruct((t, heads * MLA_V), BF16),
        compiler_params=_params("parallel", "parallel", "parallel"),
        name="mla_attention",
    )(q, k, v)


def _na_block_layout(rows):
    nb = rows // NA_ROWS_PER_BLOCK
    kh = min(NA_KH, rows)
    classes = []
    for blk in (0, 1, nb - 1):
        r0 = blk * NA_ROWS_PER_BLOCK
        ks = min(max(r0 - NA_KH // 2, 0), rows - NA_KEY_ROWS)
        entries = []
        for rq in range(NA_ROWS_PER_BLOCK):
            r = r0 + rq
            rs = min(max(r - kh // 2, 0), rows - kh)
            for j in range(kh):
                entries.append((rq, rs + j - ks, rs + j - r + NA_KH - 1))
        classes.append(entries)
    return classes


def _na_attn_kernel(q_ref, k_ref, v_ref, toep_ref, o_ref, bias_ref, *, rows, scale):
    b, step = pl.program_id(1), pl.program_id(2)
    nb = rows // NA_ROWS_PER_BLOCK
    tq = NA_ROWS_PER_BLOCK * GRID_W
    nk = NA_KEY_ROWS * GRID_W

    @pl.when((b == 0) & (step == 0))
    def _build_bias():
        bias_ref[...] = jnp.full(bias_ref.shape, -jnp.inf, F32)
        for c, entries in enumerate(_na_block_layout(rows)):
            for rq, rk, d in entries:
                bias_ref[c, rq * GRID_W:(rq + 1) * GRID_W, rk * GRID_W:(rk + 1) * GRID_W] = toep_ref[0, d]

    for sub in range(NA_BLOCKS_PER_STEP):
        blk = step * NA_BLOCKS_PER_STEP + sub
        ks = jnp.clip(blk * NA_ROWS_PER_BLOCK - NA_KH // 2, 0, rows - NA_KEY_ROWS)
        off = pl.multiple_of(ks * GRID_W, GRID_W)
        cls = jnp.where(blk == 0, 0, jnp.where(blk == nb - 1, 2, 1))
        q = q_ref[sub * tq:(sub + 1) * tq, :]
        s = lax.dot_general(q, k_ref[pl.ds(off, nk), :], NT_DIMS, preferred_element_type=F32)
        s = s * scale + bias_ref[cls]
        m = jnp.max(s, axis=1, keepdims=True)
        p = jnp.exp(s - m)
        l = jnp.sum(p, axis=1, keepdims=True)
        o = jnp.dot(p.astype(BF16), v_ref[pl.ds(off, nk), :], preferred_element_type=F32)
        o_ref[sub * tq:(sub + 1) * tq, :] = (o / l).astype(o_ref.dtype)


def _na_attn(qkv, toep, batch, seq, heads):
    t = qkv.shape[0]
    rows = seq // GRID_W
    tq = NA_ROWS_PER_BLOCK * GRID_W
    tstep = NA_BLOCKS_PER_STEP * tq
    ns = seq // tstep
    return pl.pallas_call(
        functools.partial(_na_attn_kernel, rows=rows, scale=NA_D ** -0.5),
        grid=(heads, batch, ns),
        in_specs=[pl.BlockSpec((tstep, NA_D), lambda h, b, i: (b * ns + i, h)),
                  pl.BlockSpec((seq, NA_D), lambda h, b, i: (b, heads + h)),
                  pl.BlockSpec((seq, NA_D), lambda h, b, i: (b, 2 * heads + h)),
                  pl.BlockSpec((1, 2 * NA_KH - 1, GRID_W, GRID_W), lambda h, b, i: (h, 0, 0, 0))],
        out_specs=pl.BlockSpec((tstep, NA_D), lambda h, b, i: (b * ns + i, h)),
        out_shape=jax.ShapeDtypeStruct((t, heads * NA_D), BF16),
        scratch_shapes=[pltpu.VMEM((3, tq, NA_KEY_ROWS * GRID_W), F32)],
        compiler_params=_params("arbitrary", "arbitrary", "arbitrary"),
        name="na_attention",
    )(qkv, qkv, qkv, toep)


def _gate_merge_kernel(h_ref, wga_ref, wgb_ref, ba_ref, bb_ref, oa_ref, pa_ref, ob_ref, pb_ref, o_ref):
    h = h_ref[...]
    ga = jax.nn.sigmoid(jnp.dot(h, wga_ref[...], preferred_element_type=F32) + ba_ref[...])
    ya = jnp.dot(oa_ref[...], pa_ref[...], preferred_element_type=F32)
    acc = ga * ya
    gb = jax.nn.sigmoid(jnp.dot(h, wgb_ref[...], preferred_element_type=F32) + bb_ref[...])
    yb = jnp.dot(ob_ref[...], pb_ref[...], preferred_element_type=F32)
    o_ref[...] = (acc + gb * yb).astype(o_ref.dtype)


def _gate_merge(h, w_gate, b_gate, oa, pa, ob, pb, tm=512, tn=512):
    m, d = h.shape
    wa, wb = oa.shape[1], ob.shape[1]
    tm, tn = _tile(m, tm), _tile(d, tn)
    nj = d // tn
    return pl.pallas_call(
        _gate_merge_kernel,
        grid=(m // tm, nj),
        in_specs=[pl.BlockSpec((tm, d), lambda i, j: (i, 0)),
                  pl.BlockSpec((d, tn), lambda i, j: (0, j)),
                  pl.BlockSpec((d, tn), lambda i, j: (0, nj + j)),
                  pl.BlockSpec((1, tn), lambda i, j: (0, j)),
                  pl.BlockSpec((1, tn), lambda i, j: (0, nj + j)),
                  pl.BlockSpec((tm, wa), lambda i, j: (i, 0)),
                  pl.BlockSpec((wa, tn), lambda i, j: (0, j)),
                  pl.BlockSpec((tm, wb), lambda i, j: (i, 0)),
                  pl.BlockSpec((wb, tn), lambda i, j: (0, j))],
        out_specs=pl.BlockSpec((tm, tn), lambda i, j: (i, j)),
        out_shape=jax.ShapeDtypeStruct((m, d), BF16),
        compiler_params=_params("parallel", "parallel"),
        name="gate_merge",
    )(h, w_gate, w_gate, b_gate, b_gate, oa, pa, ob, pb)


def _peer_scores_kernel(x_ref, w_ref, sk_ref, o_ref):
    q = jnp.dot(x_ref[...], w_ref[...], preferred_element_type=F32).astype(BF16)
    for p in range(2):
        o_ref[p] = lax.dot_general(sk_ref[p], q[:, p * LANE:(p + 1) * LANE], NT_DIMS,
                                   preferred_element_type=F32)


def _peer_scores(x, wq, sk, tm=1024):
    t, d = x.shape
    ph = sk.shape[0] // 2
    half = sk.shape[2]
    tm = _tile(t, tm)
    return pl.pallas_call(
        _peer_scores_kernel,
        grid=(t // tm, ph),
        in_specs=[pl.BlockSpec((tm, d), lambda i, h: (i, 0)),
                  pl.BlockSpec((d, 2 * half), lambda i, h: (0, h)),
                  pl.BlockSpec((2, N_KEYS, half), lambda i, h: (h, 0, 0))],
        out_specs=pl.BlockSpec((2, N_KEYS, tm), lambda i, h: (h, 0, i)),
        out_shape=jax.ShapeDtypeStruct((2 * ph, N_KEYS, t), F32),
        compiler_params=_params("parallel", "parallel"),
        name="peer_scores",
    )(x, wq, sk)


def _top_values(vals, n):
    rowid = lax.broadcasted_iota(jnp.int32, vals.shape, 0)
    big = jnp.int32(vals.shape[0])
    out = []
    for _ in range(n):
        m = jnp.max(vals, axis=0, keepdims=True)
        first = jnp.min(jnp.where(vals == m, rowid, big), axis=0, keepdims=True)
        vals = jnp.where(rowid == first, -jnp.inf, vals)
        out.append(m)
    return out


def _peer_topk_kernel(s_ref, e1_ref, e2_ref, tau_ref):
    s1, s2 = s_ref[0], s_ref[1]
    v1 = _top_values(s1, PEER_TOPK)
    v2 = _top_values(s2, PEER_TOPK)
    rows = [v1[a] + v2[b] for a in range(PEER_TOPK) for b in range(PEER_TOPK // (a + 1))]
    rows += [jnp.full_like(rows[0], -jnp.inf)] * (-len(rows) % 8)
    c = _top_values(jnp.concatenate(rows, axis=0), PEER_TOPK)
    z = c[0] * 0.0
    for ck in c:
        z = z + jnp.exp(ck - c[0])
    tau_ref[0] = c[PEER_TOPK - 1]
    e1_ref[0] = jnp.exp(s1 - v1[0])
    e2_ref[0] = jnp.exp(s2 - v2[0]) / z


def _peer_topk(st, tt=256):
    g, n, t = st.shape
    ph = g // 2
    tt = _tile(t, tt)
    tab = jax.ShapeDtypeStruct((ph, n, t), F32)
    return pl.pallas_call(
        _peer_topk_kernel,
        grid=(ph, t // tt),
        in_specs=[pl.BlockSpec((2, n, tt), lambda h, i: (h, 0, i))],
        out_specs=[pl.BlockSpec((1, n, tt), lambda h, i: (h, 0, i)),
                   pl.BlockSpec((1, n, tt), lambda h, i: (h, 0, i)),
                   pl.BlockSpec((1, 1, tt), lambda h, i: (h, 0, i))],
        out_shape=[tab, tab, jax.ShapeDtypeStruct((ph, 1, t), F32)],
        compiler_params=_params("parallel", "parallel"),
        name="peer_topk",
    )(st)


def _peer_main_kernel(xt_ref, u_ref, vt_ref, s1_ref, s2_ref, e1_ref, e2_ref, tau_ref, o_ref, g_ref, at_ref,
                      *, te, ph, n_e):
    e = pl.program_id(1)
    d, tm = o_ref.shape
    groups = te // N_KEYS
    tile = jnp.minimum(e, n_e - 1)

    @pl.when(e == 0)
    def _init():
        o_ref[...] = jnp.zeros(o_ref.shape, o_ref.dtype)
        at_ref[...] = jnp.zeros(at_ref.shape, at_ref.dtype)

    def gates(c, zero):
        i = tile * groups + c
        s1 = [s1_ref[h, pl.ds(i, 1), :] for h in range(ph)]
        e1 = [e1_ref[h, pl.ds(i, 1), :] for h in range(ph)]
        gl = min(tm, PEER_GATE_LANES)
        for l0 in range(0, tm, gl):
            lanes = slice(l0, l0 + gl)
            g = None
            for h in range(ph):
                sel = (s1[h][:, lanes] + s2_ref[h, :, lanes]) >= tau_ref[h, :, lanes] + zero[:, lanes]
                w = jnp.where(sel, e1[h][:, lanes] * e2_ref[h, :, lanes], 0.0)
                g = w if g is None else g + w
            g_ref[c * N_KEYS:(c + 1) * N_KEYS, lanes] = g

    o_ref[...] += jnp.dot(vt_ref[...], at_ref[...], preferred_element_type=F32)
    bits = lax.bitcast_convert_type(o_ref[d - 1:d, :], jnp.int32)
    zero = lax.shift_right_logical(lax.shift_right_logical(bits, 31), 1).astype(F32)
    for c in range(groups):
        gates(c, zero)

    u_rows = min(te, PEER_IN_ROWS)
    for r in range(te // u_rows):
        rows = slice(r * u_rows, (r + 1) * u_rows)
        ht = jnp.dot(u_ref[rows, :], xt_ref[...], preferred_element_type=F32)
        act = 0.5 * ht * (1.0 + lax.erf(ht * (2.0 ** -0.5)))
        at_ref[rows, :] = (g_ref[rows, :] * act).astype(at_ref.dtype)


def _peer_main(xt, u, vt, st4, e1, e2, tau, tm=512, te=512):
    d, t = xt.shape
    n_exp = u.shape[0]
    ph = e1.shape[0]
    tm, te = _tile(t, tm), _tile(n_exp, te)
    n_e = n_exp // te
    once = pl.Buffered(1)
    return pl.pallas_call(
        functools.partial(_peer_main_kernel, te=te, ph=ph, n_e=n_e),
        grid=(t // tm, n_e + 1),
        in_specs=[pl.BlockSpec((d, tm), lambda i, e: (0, i), pipeline_mode=once),
                  pl.BlockSpec((te, d), lambda i, e: (jnp.minimum(e, n_e - 1), 0)),
                  pl.BlockSpec((d, te), lambda i, e: (0, jnp.maximum(e - 1, 0))),
                  pl.BlockSpec((ph, None, N_KEYS, tm), lambda i, e: (0, 0, 0, i), pipeline_mode=once),
                  pl.BlockSpec((ph, None, N_KEYS, tm), lambda i, e: (0, 1, 0, i), pipeline_mode=once),
                  pl.BlockSpec((ph, N_KEYS, tm), lambda i, e: (0, 0, i), pipeline_mode=once),
                  pl.BlockSpec((ph, N_KEYS, tm), lambda i, e: (0, 0, i), pipeline_mode=once),
                  pl.BlockSpec((ph, 1, tm), lambda i, e: (0, 0, i), pipeline_mode=once)],
        out_specs=pl.BlockSpec((d, tm), lambda i, e: (0, i)),
        out_shape=jax.ShapeDtypeStruct((d, t), F32),
        scratch_shapes=[pltpu.VMEM((te, tm), F32), pltpu.VMEM((te, tm), BF16)],
        compiler_params=_params("parallel", "arbitrary"),
        name="peer_experts",
    )(xt, u, vt, st4, st4, e1, e2, tau)


def _final_kernel(x_ref, pt_ref, g_ref, o_ref):
    o_ref[...] = _rms(x_ref[...] + pt_ref[...].T, g_ref[...])


def _final(x1, pt, g, tm=256):
    t, d = x1.shape
    tm = _tile(t, tm)
    return pl.pallas_call(
        _final_kernel,
        grid=(t // tm,),
        in_specs=[pl.BlockSpec((tm, d), lambda i: (i, 0)),
                  pl.BlockSpec((d, tm), lambda i: (0, i)),
                  pl.BlockSpec((1, d), lambda i: (0, 0))],
        out_specs=pl.BlockSpec((tm, d), lambda i: (i, 0)),
        out_shape=jax.ShapeDtypeStruct((t, d), F32),
        compiler_params=_params("parallel"),
        name="final_norm",
    )(x1, pt, g.reshape(1, d))


def _rot_cols(w):
    q = ROPE // 4
    idx = np.concatenate([np.arange(q, 2 * q), np.arange(0, q), np.arange(3 * q, 4 * q), np.arange(2 * q, 3 * q)])
    sign = np.concatenate([-np.ones(q), np.ones(q), -np.ones(q), np.ones(q)]).astype(np.float32)
    return w[..., idx] * sign


def _rope_tables(seq):
    t = jnp.arange(seq)
    half = ROPE // 4
    freqs = ROPE_BASE ** (-jnp.arange(half, dtype=F32) / half)
    ang_r = (t // GRID_W).astype(F32)[:, None] * freqs[None, :]
    ang_c = (t % GRID_W).astype(F32)[:, None] * freqs[None, :]
    zeros = jnp.zeros((seq, LANE - ROPE), F32)
    cos = jnp.concatenate([jnp.cos(ang_r), jnp.cos(ang_r), jnp.cos(ang_c), jnp.cos(ang_c), zeros], axis=1)
    sin = jnp.concatenate([jnp.sin(ang_r), jnp.sin(ang_r), jnp.sin(ang_c), jnp.sin(ang_c), zeros], axis=1)
    return cos, sin


def _na_toeplitz(rpb):
    cols = np.arange(GRID_W)
    start = np.clip(cols - NA_KW // 2, 0, GRID_W - NA_KW)
    ck = cols[None, :]
    inside = (ck >= start[:, None]) & (ck < start[:, None] + NA_KW)
    dc = np.clip(ck - cols[:, None] + NA_KW - 1, 0, 2 * NA_KW - 2)
    return jnp.where(inside[None, None], rpb[:, :, dc], -jnp.inf)


def kernel(x, attn_norm, w_in, q_norm, w_uq, kv_norm, w_ukv, w_branch_mla, na_rpb, w_branch_na,
           w_gate, b_gate, w_out, ffn_norm, peer_w_query, peer_subkeys, peer_u, peer_v, final_norm):
    batch, seq, d = x.shape
    t = batch * seq
    depth = w_in.shape[0]
    q_rank, kv_rank = q_norm.shape[1], kv_norm.shape[1]
    mla_heads = w_uq.shape[2] // MLA_QK
    na_heads = na_rpb.shape[1]
    na_w = na_heads * NA_D
    cos, sin = _rope_tables(seq)
    xf = x.reshape(t, d)
    for l in range(depth):
        o1, o2, o3 = q_rank, q_rank + kv_rank, q_rank + kv_rank + ROPE
        w_na = w_in[l][:, o3:].astype(BF16)
        cq_w = -(-q_rank // kv_rank) * kv_rank
        pad = jnp.zeros((d, cq_w - q_rank), F32)
        w_kr = w_in[l][:, o2:o3]
        w_lat = jnp.concatenate([w_in[l][:, :o1], pad, w_in[l][:, o1:o2], w_kr, _rot_cols(w_kr)], axis=1).astype(BF16)
        ckv_blk = cq_w // kv_rank
        kr_blk = (cq_w + kv_rank) // LANE

        wq3 = w_uq[l].reshape(q_rank, mla_heads, MLA_QK)
        wq = jnp.concatenate([wq3[..., :NOPE], wq3[..., NOPE:], _rot_cols(wq3[..., NOPE:])], axis=-1)
        wq = wq.reshape(q_rank, mla_heads * QK_PAD).astype(BF16)
        wkv3 = w_ukv[l].reshape(kv_rank, mla_heads, NOPE + MLA_V)
        wk = wkv3[..., :NOPE].reshape(kv_rank, mla_heads * NOPE).astype(BF16)
        wv = wkv3[..., NOPE:].reshape(kv_rank, mla_heads * MLA_V).astype(BF16)

        h = _rmsnorm_cast(xf, attn_norm[l])
        qkv = _matmul(h, w_na, BF16, 1024, 1024, name="na_proj")
        lat = _matmul(h, w_lat, F32, 512, w_lat.shape[1], name="mla_latent_proj")
        q_cat = _mla_q(lat, q_norm[l], wq, cos, sin, mla_heads, seq)
        k_cat, v_mla = _mla_kv(lat, ckv_blk, kr_blk, kv_norm[l], wk, wv, cos, sin, mla_heads, seq)
        o_a = _mla_attn(q_cat, k_cat, v_mla, batch, seq, mla_heads)
        o_b = _na_attn(qkv, _na_toeplitz(na_rpb[l]), batch, seq, na_heads)
        merged = _gate_merge(h, w_gate[l].astype(BF16), b_gate[l].reshape(1, -1), o_a,
                             w_branch_mla[l].astype(BF16), o_b, w_branch_na[l].astype(BF16))
        xf = _matmul(merged, w_out[l].astype(BF16), F32, 512, 1024, residual=xf, name="out_proj")

        h2, h2t = _rmsnorm_cast(xf, ffn_norm[l], transposed=True)
        ph = peer_subkeys.shape[1]
        sk = peer_subkeys[l].reshape(2 * ph, N_KEYS, -1).astype(BF16)
        st = _peer_scores(h2, peer_w_query[l].astype(BF16), sk)
        e1, e2, tau = _peer_topk(st)
        peer_t = _peer_main(h2t, peer_u[l].astype(BF16), peer_v[l].T.astype(BF16),
                            st.reshape(ph, 2, N_KEYS, t), e1, e2, tau)
        if l + 1 < depth:
            xf = xf + peer_t.T
    return _final(xf, peer_t, final_norm).reshape(batch, seq, d)
```

```python
import functools
import math

import jax
import jax.numpy as jnp
import numpy as np
from jax import lax
from jax.experimental import pallas as pl
from jax.experimental.pallas import tpu as pltpu

F32 = jnp.float32
BF16 = jnp.bfloat16

EPS = 1e-6
ROPE_BASE = 10000.0
GRID_W = 64
NOPE = 128
ROPE = 64
MLA_V = 128
MLA_QK = NOPE + ROPE
QK_PAD = 256
V_PAD = 256
NA_D = 128
NA_KH = 8
NA_KW = 16
NA_ROWS_PER_BLOCK = 4
NA_KEY_ROWS = 12
NA_BLOCKS_PER_STEP = 4
N_KEYS = 128
PEER_TOPK = 16
LANE = 128
LOG2E = math.log2(math.e)
PEER_GATE_LANES = 128
VMEM_LIMIT = 56 * 1024 * 1024

NT_DIMS = (((1,), (1,)), ((), ()))


def _params(*sem):
    return pltpu.CompilerParams(dimension_semantics=sem, vmem_limit_bytes=VMEM_LIMIT)


def _tile(n, pref):
    t = min(n, pref)
    while n % t:
        t //= 2
    return t


def _rms(x, g):
    return x * lax.rsqrt(jnp.mean(x * x, axis=-1, keepdims=True) + EPS) * g


def _rmsnorm_kernel(x_ref, g_ref, o_ref, *ot_ref):
    y = _rms(x_ref[...], g_ref[...])
    o_ref[...] = y.astype(o_ref.dtype)
    for r in ot_ref:
        r[...] = y.T.astype(r.dtype)


def _rmsnorm_cast(x, g, tm=256, transposed=False):
    m, d = x.shape
    tm = _tile(m, tm)
    out_specs = [pl.BlockSpec((tm, d), lambda i: (i, 0))]
    out_shape = [jax.ShapeDtypeStruct((m, d), BF16)]
    if transposed:
        out_specs.append(pl.BlockSpec((d, tm), lambda i: (0, i)))
        out_shape.append(jax.ShapeDtypeStruct((d, m), BF16))
    out = pl.pallas_call(
        _rmsnorm_kernel,
        grid=(m // tm,),
        in_specs=[pl.BlockSpec((tm, d), lambda i: (i, 0)),
                  pl.BlockSpec((1, d), lambda i: (0, 0))],
        out_specs=out_specs,
        out_shape=out_shape,
        compiler_params=_params("parallel"),
        name="rmsnorm_cast",
    )(x, g.reshape(1, d))
    return out if transposed else out[0]


def _mm_kernel(a_ref, w_ref, o_ref):
    o_ref[...] = jnp.dot(a_ref[...], w_ref[...], preferred_element_type=F32).astype(o_ref.dtype)


def _mm_res_kernel(a_ref, w_ref, r_ref, o_ref):
    acc = jnp.dot(a_ref[...], w_ref[...], preferred_element_type=F32)
    o_ref[...] = (r_ref[...] + acc).astype(o_ref.dtype)


def _matmul(a, w, out_dtype, tm, tn, residual=None, name="matmul"):
    m, k = a.shape
    n = w.shape[1]
    tm, tn = _tile(m, tm), _tile(n, tn)
    in_specs = [pl.BlockSpec((tm, k), lambda i, j: (i, 0)),
                pl.BlockSpec((k, tn), lambda i, j: (0, j))]
    args = [a, w]
    body = _mm_kernel
    if residual is not None:
        in_specs.append(pl.BlockSpec((tm, tn), lambda i, j: (i, j)))
        args.append(residual)
        body = _mm_res_kernel
    return pl.pallas_call(
        body,
        grid=(m // tm, n // tn),
        in_specs=in_specs,
        out_specs=pl.BlockSpec((tm, tn), lambda i, j: (i, j)),
        out_shape=jax.ShapeDtypeStruct((m, n), out_dtype),
        compiler_params=_params("parallel", "parallel"),
        name=name,
    )(*args)


def _rope_pair(blk, cos, sin):
    return blk * cos + pltpu.roll(blk, ROPE, 1) * sin


def _mla_q_kernel(c_ref, g_ref, w_ref, cos_ref, sin_ref, o_ref, *, heads, scale):
    cn = _rms(c_ref[...], g_ref[...]).astype(BF16)
    cos, sin = cos_ref[...], sin_ref[...]
    for h in range(heads):
        y = jnp.dot(cn, w_ref[:, h * QK_PAD:(h + 1) * QK_PAD], preferred_element_type=F32) * scale
        o_ref[:, h * QK_PAD:h * QK_PAD + NOPE] = y[:, :NOPE].astype(o_ref.dtype)
        o_ref[:, h * QK_PAD + NOPE:(h + 1) * QK_PAD] = _rope_pair(y[:, NOPE:], cos, sin).astype(o_ref.dtype)


def _mla_q(lat, g, wq, cos, sin, heads, seq, tm=512):
    t = lat.shape[0]
    r = g.shape[0]
    tm = _tile(seq, tm)
    sb = seq // tm
    return pl.pallas_call(
        functools.partial(_mla_q_kernel, heads=heads, scale=MLA_QK ** -0.5 * LOG2E),
        grid=(t // tm,),
        in_specs=[pl.BlockSpec((tm, r), lambda i: (i, 0)),
                  pl.BlockSpec((1, r), lambda i: (0, 0)),
                  pl.BlockSpec((r, heads * QK_PAD), lambda i: (0, 0)),
                  pl.BlockSpec((tm, LANE), lambda i: (i % sb, 0)),
                  pl.BlockSpec((tm, LANE), lambda i: (i % sb, 0))],
        out_specs=pl.BlockSpec((tm, heads * QK_PAD), lambda i: (i, 0)),
        out_shape=jax.ShapeDtypeStruct((t, heads * QK_PAD), BF16),
        compiler_params=_params("parallel"),
        name="mla_q_proj",
    )(lat, g.reshape(1, r), wq, cos, sin)


def _mla_kv_kernel(c_ref, kr_ref, g_ref, wk_ref, wv_ref, cos_ref, sin_ref, k_ref, v_ref, *, heads):
    cn = _rms(c_ref[...], g_ref[...]).astype(BF16)
    kpe = _rope_pair(kr_ref[...], cos_ref[...], sin_ref[...]).astype(k_ref.dtype)
    kn = jnp.dot(cn, wk_ref[...], preferred_element_type=F32).astype(k_ref.dtype)
    v = jnp.dot(cn, wv_ref[...], preferred_element_type=F32).astype(v_ref.dtype)
    lane = lax.broadcasted_iota(jnp.int32, (cn.shape[0], V_PAD - MLA_V), 1)
    ones_col = jnp.where(lane == 0, 1.0, 0.0).astype(v_ref.dtype)
    for h in range(heads):
        k_ref[:, h * QK_PAD:h * QK_PAD + NOPE] = kn[:, h * NOPE:(h + 1) * NOPE]
        k_ref[:, h * QK_PAD + NOPE:(h + 1) * QK_PAD] = kpe
        v_ref[:, h * V_PAD:h * V_PAD + MLA_V] = v[:, h * MLA_V:(h + 1) * MLA_V]
        v_ref[:, h * V_PAD + MLA_V:(h + 1) * V_PAD] = ones_col


def _mla_kv(lat, ckv_blk, kr_blk, g, wk, wv, cos, sin, heads, seq, tm=512):
    t = lat.shape[0]
    r = g.shape[0]
    tm = _tile(seq, tm)
    sb = seq // tm
    return pl.pallas_call(
        functools.partial(_mla_kv_kernel, heads=heads),
        grid=(t // tm,),
        in_specs=[pl.BlockSpec((tm, r), lambda i: (i, ckv_blk)),
                  pl.BlockSpec((tm, LANE), lambda i: (i, kr_blk)),
                  pl.BlockSpec((1, r), lambda i: (0, 0)),
                  pl.BlockSpec((r, heads * NOPE), lambda i: (0, 0)),
                  pl.BlockSpec((r, heads * MLA_V), lambda i: (0, 0)),
                  pl.BlockSpec((tm, LANE), lambda i: (i % sb, 0)),
                  pl.BlockSpec((tm, LANE), lambda i: (i % sb, 0))],
        out_specs=[pl.BlockSpec((tm, heads * QK_PAD), lambda i: (i, 0)),
                   pl.BlockSpec((tm, heads * V_PAD), lambda i: (i, 0))],
        out_shape=[jax.ShapeDtypeStruct((t, heads * QK_PAD), BF16),
                   jax.ShapeDtypeStruct((t, heads * V_PAD), BF16)],
        compiler_params=_params("parallel"),
        name="mla_kv_proj",
    )(lat, lat, g.reshape(1, r), wk, wv, cos, sin)


def _mla_attn_kernel(q_ref, k_ref, v_ref, o_ref, *, tk):
    q = q_ref[...]
    m = acc = None
    for j in range(k_ref.shape[0] // tk):
        s = lax.dot_general(q, k_ref[j * tk:(j + 1) * tk, :], NT_DIMS, preferred_element_type=F32)
        m_blk = jnp.max(s, axis=1, keepdims=True)
        m_new = m_blk if j == 0 else jnp.maximum(m, m_blk)
        p = jnp.exp2(s - m_new).astype(BF16)
        pv = jnp.dot(p, v_ref[j * tk:(j + 1) * tk, :], preferred_element_type=F32)
        acc = pv if j == 0 else jnp.exp2(m - m_new) * acc + pv
        m = m_new
    o_ref[...] = (acc[:, :MLA_V] / acc[:, MLA_V:MLA_V + 1]).astype(o_ref.dtype)


def _mla_attn(q, k, v, batch, seq, heads, tq=1024, tk=1024):
    t = q.shape[0]
    tq, tk = _tile(seq, tq), _tile(seq, tk)
    qb = seq // tq
    return pl.pallas_call(
        functools.partial(_mla_attn_kernel, tk=tk),
        grid=(batch, heads, qb),
        in_specs=[pl.BlockSpec((tq, QK_PAD), lambda b, h, i: (b * qb + i, h)),
                  pl.BlockSpec((seq, QK_PAD), lambda b, h, i: (b, h)),
                  pl.BlockSpec((seq, V_PAD), lambda b, h, i: (b, h))],
        out_specs=pl.BlockSpec((tq, MLA_V), lambda b, h, i: (b * qb + i, h)),
        out_shape=jax.ShapeDtypeStruct((t, heads * MLA_V), BF16),
        compiler_params=_params("parallel", "parallel", "parallel"),
        name="mla_attention",
    )(q, k, v)


def _na_block_layout(rows):
    nb = rows // NA_ROWS_PER_BLOCK
    kh = min(NA_KH, rows)
    classes = []
    for blk in (0, 1, nb - 1):
        r0 = blk * NA_ROWS_PER_BLOCK
        ks = min(max(r0 - NA_KH // 2, 0), rows - NA_KEY_ROWS)
        entries = []
        for rq in range(NA_ROWS_PER_BLOCK):
            r = r0 + rq
            rs = min(max(r - kh // 2, 0), rows - kh)
            for j in range(kh):
                entries.append((rq, rs + j - ks, rs + j - r + NA_KH - 1))
        classes.append(entries)
    return classes


def _na_attn_kernel(q_ref, k_ref, v_ref, toep_ref, o_ref, bias_ref, *, rows, scale):
    b, step = pl.program_id(1), pl.program_id(2)
    nb = rows // NA_ROWS_PER_BLOCK
    tq = NA_ROWS_PER_BLOCK * GRID_W
    nk = NA_KEY_ROWS * GRID_W

    @pl.when((b == 0) & (step == 0))
    def _build_bias():
        bias_ref[...] = jnp.full(bias_ref.shape, -jnp.inf, F32)
        for c, entries in enumerate(_na_block_layout(rows)):
            for rq, rk, d in entries:
                bias_ref[c, rq * GRID_W:(rq + 1) * GRID_W, rk * GRID_W:(rk + 1) * GRID_W] = toep_ref[0, d] * LOG2E

    for sub in range(NA_BLOCKS_PER_STEP):
        blk = step * NA_BLOCKS_PER_STEP + sub
        ks = jnp.clip(blk * NA_ROWS_PER_BLOCK - NA_KH // 2, 0, rows - NA_KEY_ROWS)
        off = pl.multiple_of(ks * GRID_W, GRID_W)
        cls = jnp.where(blk == 0, 0, jnp.where(blk == nb - 1, 2, 1))
        q = q_ref[sub * tq:(sub + 1) * tq, :]
        s = lax.dot_general(q, k_ref[pl.ds(off, nk), :], NT_DIMS, preferred_element_type=F32)
        s = s * scale + bias_ref[cls]
        m = jnp.max(s, axis=1, keepdims=True)
        p = jnp.exp2(s - m)
        l = jnp.sum(p, axis=1, keepdims=True)
        o = jnp.dot(p.astype(BF16), v_ref[pl.ds(off, nk), :], preferred_element_type=F32)
        o_ref[sub * tq:(sub + 1) * tq, :] = (o / l).astype(o_ref.dtype)


def _na_attn(qkv, toep, batch, seq, heads):
    t = qkv.shape[0]
    rows = seq // GRID_W
    tq = NA_ROWS_PER_BLOCK * GRID_W
    tstep = NA_BLOCKS_PER_STEP * tq
    ns = seq // tstep
    return pl.pallas_call(
        functools.partial(_na_attn_kernel, rows=rows, scale=NA_D ** -0.5 * LOG2E),
        grid=(heads, batch, ns),
        in_specs=[pl.BlockSpec((tstep, NA_D), lambda h, b, i: (b * ns + i, h)),
                  pl.BlockSpec((seq, NA_D), lambda h, b, i: (b, heads + h)),
                  pl.BlockSpec((seq, NA_D), lambda h, b, i: (b, 2 * heads + h)),
                  pl.BlockSpec((1, 2 * NA_KH - 1, GRID_W, GRID_W), lambda h, b, i: (h, 0, 0, 0))],
        out_specs=pl.BlockSpec((tstep, NA_D), lambda h, b, i: (b * ns + i, h)),
        out_shape=jax.ShapeDtypeStruct((t, heads * NA_D), BF16),
        scratch_shapes=[pltpu.VMEM((3, tq, NA_KEY_ROWS * GRID_W), F32)],
        compiler_params=_params("arbitrary", "arbitrary", "arbitrary"),
        name="na_attention",
    )(qkv, qkv, qkv, toep)


def _gate_merge_kernel(h_ref, wga_ref, wgb_ref, ba_ref, bb_ref, oa_ref, pa_ref, ob_ref, pb_ref, o_ref):
    h = h_ref[...]
    ga = jax.nn.sigmoid(jnp.dot(h, wga_ref[...], preferred_element_type=F32) + ba_ref[...])
    ya = jnp.dot(oa_ref[...], pa_ref[...], preferred_element_type=F32)
    acc = ga * ya
    gb = jax.nn.sigmoid(jnp.dot(h, wgb_ref[...], preferred_element_type=F32) + bb_ref[...])
    yb = jnp.dot(ob_ref[...], pb_ref[...], preferred_element_type=F32)
    o_ref[...] = (acc + gb * yb).astype(o_ref.dtype)


def _gate_merge(h, w_gate, b_gate, oa, pa, ob, pb, tm=512, tn=512):
    m, d = h.shape
    wa, wb = oa.shape[1], ob.shape[1]
    tm, tn = _tile(m, tm), _tile(d, tn)
    nj = d // tn
    return pl.pallas_call(
        _gate_merge_kernel,
        grid=(m // tm, nj),
        in_specs=[pl.BlockSpec((tm, d), lambda i, j: (i, 0)),
                  pl.BlockSpec((d, tn), lambda i, j: (0, j)),
                  pl.BlockSpec((d, tn), lambda i, j: (0, nj + j)),
                  pl.BlockSpec((1, tn), lambda i, j: (0, j)),
                  pl.BlockSpec((1, tn), lambda i, j: (0, nj + j)),
                  pl.BlockSpec((tm, wa), lambda i, j: (i, 0)),
                  pl.BlockSpec((wa, tn), lambda i, j: (0, j)),
                  pl.BlockSpec((tm, wb), lambda i, j: (i, 0)),
                  pl.BlockSpec((wb, tn), lambda i, j: (0, j))],
        out_specs=pl.BlockSpec((tm, tn), lambda i, j: (i, j)),
        out_shape=jax.ShapeDtypeStruct((m, d), BF16),
        compiler_params=_params("parallel", "parallel"),
        name="gate_merge",
    )(h, w_gate, w_gate, b_gate, b_gate, oa, pa, ob, pb)


def _peer_scores_kernel(x_ref, w_ref, sk_ref, o_ref):
    q = jnp.dot(x_ref[...], w_ref[...], preferred_element_type=F32).astype(BF16)
    for p in range(2):
        o_ref[p] = lax.dot_general(sk_ref[p], q[:, p * LANE:(p + 1) * LANE], NT_DIMS,
                                   preferred_element_type=F32)


def _peer_scores(x, wq, sk, tm=1024):
    t, d = x.shape
    ph = sk.shape[0] // 2
    half = sk.shape[2]
    tm = _tile(t, tm)
    return pl.pallas_call(
        _peer_scores_kernel,
        grid=(t // tm, ph),
        in_specs=[pl.BlockSpec((tm, d), lambda i, h: (i, 0)),
                  pl.BlockSpec((d, 2 * half), lambda i, h: (0, h)),
                  pl.BlockSpec((2, N_KEYS, half), lambda i, h: (h, 0, 0))],
        out_specs=pl.BlockSpec((2, N_KEYS, tm), lambda i, h: (h, 0, i)),
        out_shape=jax.ShapeDtypeStruct((2 * ph, N_KEYS, t), F32),
        compiler_params=_params("parallel", "parallel"),
        name="peer_scores",
    )(x, wq, sk)


def _top_values(vals, n):
    rowid = lax.broadcasted_iota(jnp.int32, vals.shape, 0)
    big = jnp.int32(vals.shape[0])
    out = []
    for _ in range(n):
        m = jnp.max(vals, axis=0, keepdims=True)
        first = jnp.min(jnp.where(vals == m, rowid, big), axis=0, keepdims=True)
        vals = jnp.where(rowid == first, -jnp.inf, vals)
        out.append(m)
    return out


def _peer_topk_kernel(s_ref, e1_ref, e2_ref, tau_ref):
    s1, s2 = s_ref[0], s_ref[1]
    v1 = _top_values(s1, PEER_TOPK)
    v2 = _top_values(s2, PEER_TOPK)
    rows = [v1[a] + v2[b] for a in range(PEER_TOPK) for b in range(PEER_TOPK // (a + 1))]
    rows += [jnp.full_like(rows[0], -jnp.inf)] * (-len(rows) % 8)
    c = _top_values(jnp.concatenate(rows, axis=0), PEER_TOPK)
    z = c[0] * 0.0
    for ck in c:
        z = z + jnp.exp(ck - c[0])
    tau_ref[0] = c[PEER_TOPK - 1]
    e1_ref[0] = jnp.exp(s1 - v1[0])
    e2_ref[0] = jnp.exp(s2 - v2[0]) / z


def _peer_topk(st, tt=256):
    g, n, t = st.shape
    ph = g // 2
    tt = _tile(t, tt)
    tab = jax.ShapeDtypeStruct((ph, n, t), F32)
    return pl.pallas_call(
        _peer_topk_kernel,
        grid=(ph, t // tt),
        in_specs=[pl.BlockSpec((2, n, tt), lambda h, i: (h, 0, i))],
        out_specs=[pl.BlockSpec((1, n, tt), lambda h, i: (h, 0, i)),
                   pl.BlockSpec((1, n, tt), lambda h, i: (h, 0, i)),
                   pl.BlockSpec((1, 1, tt), lambda h, i: (h, 0, i))],
        out_shape=[tab, tab, jax.ShapeDtypeStruct((ph, 1, t), F32)],
        compiler_params=_params("parallel", "parallel"),
        name="peer_topk",
    )(st)


def _peer_main_kernel(xt_ref, u_ref, vt_ref, s1_ref, s2_ref, e1_ref, e2_ref, tau_ref, o_ref, g_ref, at_ref,
                      *, te, ph, n_e):
    e = pl.program_id(1)
    d, tm = o_ref.shape
    groups = te // N_KEYS

    def gates(c, zero):
        i = e * groups + c
        s1 = [s1_ref[h, pl.ds(i, 1), :] for h in range(ph)]
        e1 = [e1_ref[h, pl.ds(i, 1), :] for h in range(ph)]
        gl = min(tm, PEER_GATE_LANES)
        for l0 in range(0, tm, gl):
            lanes = slice(l0, l0 + gl)
            g = None
            for h in range(ph):
                sel = (s1[h][:, lanes] + s2_ref[h, :, lanes]) >= tau_ref[h, :, lanes] + zero[:, lanes]
                w = jnp.where(sel, e1[h][:, lanes] * e2_ref[h, :, lanes], 0.0)
                g = w if g is None else g + w
            g_ref[c * N_KEYS:(c + 1) * N_KEYS, lanes] = g

    def step(first, last):
        if first:
            o_ref[...] = jnp.zeros(o_ref.shape, o_ref.dtype)
        else:
            o_ref[...] += jnp.dot(vt_ref[...], at_ref[...], preferred_element_type=F32)
        if last:
            return
        bits = lax.bitcast_convert_type(o_ref[d - 1:d, :], jnp.int32)
        zero = lax.shift_right_logical(lax.shift_right_logical(bits, 31), 1).astype(F32)
        for c in range(groups):
            gates(c, zero)
        ht = jnp.dot(u_ref[...], xt_ref[...], preferred_element_type=F32)
        act = 0.5 * ht * (1.0 + lax.erf(ht * (2.0 ** -0.5)))
        at_ref[...] = (g_ref[...] * act).astype(at_ref.dtype)

    pl.when(e == 0)(functools.partial(step, True, False))
    pl.when((e > 0) & (e < n_e))(functools.partial(step, False, False))
    pl.when(e == n_e)(functools.partial(step, False, True))


def _peer_main(xt, u, vt, st4, e1, e2, tau, tm=512, te=512):
    d, t = xt.shape
    n_exp = u.shape[0]
    ph = e1.shape[0]
    tm, te = _tile(t, tm), _tile(n_exp, te)
    n_e = n_exp // te
    once = pl.Buffered(1)
    return pl.pallas_call(
        functools.partial(_peer_main_kernel, te=te, ph=ph, n_e=n_e),
        grid=(t // tm, n_e + 1),
        in_specs=[pl.BlockSpec((d, tm), lambda i, e: (0, i), pipeline_mode=once),
                  pl.BlockSpec((te, d), lambda i, e: (jnp.minimum(e, n_e - 1), 0)),
                  pl.BlockSpec((d, te), lambda i, e: (0, jnp.maximum(e - 1, 0))),
                  pl.BlockSpec((ph, None, N_KEYS, tm), lambda i, e: (0, 0, 0, i), pipeline_mode=once),
                  pl.BlockSpec((ph, None, N_KEYS, tm), lambda i, e: (0, 1, 0, i), pipeline_mode=once),
                  pl.BlockSpec((ph, N_KEYS, tm), lambda i, e: (0, 0, i), pipeline_mode=once),
                  pl.BlockSpec((ph, N_KEYS, tm), lambda i, e: (0, 0, i), pipeline_mode=once),
                  pl.BlockSpec((ph, 1, tm), lambda i, e: (0, 0, i), pipeline_mode=once)],
        out_specs=pl.BlockSpec((d, tm), lambda i, e: (0, i)),
        out_shape=jax.ShapeDtypeStruct((d, t), F32),
        scratch_shapes=[pltpu.VMEM((te, tm), F32), pltpu.VMEM((te, tm), BF16)],
        compiler_params=_params("parallel", "arbitrary"),
        name="peer_experts",
    )(xt, u, vt, st4, st4, e1, e2, tau)


def _final_kernel(x_ref, pt_ref, g_ref, o_ref):
    o_ref[...] = _rms(x_ref[...] + pt_ref[...].T, g_ref[...])


def _final(x1, pt, g, tm=256):
    t, d = x1.shape
    tm = _tile(t, tm)
    return pl.pallas_call(
        _final_kernel,
        grid=(t // tm,),
        in_specs=[pl.BlockSpec((tm, d), lambda i: (i, 0)),
                  pl.BlockSpec((d, tm), lambda i: (0, i)),
                  pl.BlockSpec((1, d), lambda i: (0, 0))],
        out_specs=pl.BlockSpec((tm, d), lambda i: (i, 0)),
        out_shape=jax.ShapeDtypeStruct((t, d), F32),
        compiler_params=_params("parallel"),
        name="final_norm",
    )(x1, pt, g.reshape(1, d))


def _rot_cols(w):
    q = ROPE // 4
    idx = np.concatenate([np.arange(q, 2 * q), np.arange(0, q), np.arange(3 * q, 4 * q), np.arange(2 * q, 3 * q)])
    sign = np.concatenate([-np.ones(q), np.ones(q), -np.ones(q), np.ones(q)]).astype(np.float32)
    return w[..., idx] * sign


def _rope_tables(seq):
    t = jnp.arange(seq)
    half = ROPE // 4
    freqs = ROPE_BASE ** (-jnp.arange(half, dtype=F32) / half)
    ang_r = (t // GRID_W).astype(F32)[:, None] * freqs[None, :]
    ang_c = (t % GRID_W).astype(F32)[:, None] * freqs[None, :]
    zeros = jnp.zeros((seq, LANE - ROPE), F32)
    cos = jnp.concatenate([jnp.cos(ang_r), jnp.cos(ang_r), jnp.cos(ang_c), jnp.cos(ang_c), zeros], axis=1)
    sin = jnp.concatenate([jnp.sin(ang_r), jnp.sin(ang_r), jnp.sin(ang_c), jnp.sin(ang_c), zeros], axis=1)
    return cos, sin


def _na_toeplitz(rpb):
    cols = np.arange(GRID_W)
    start = np.clip(cols - NA_KW // 2, 0, GRID_W - NA_KW)
    ck = cols[None, :]
    inside = (ck >= start[:, None]) & (ck < start[:, None] + NA_KW)
    dc = np.clip(ck - cols[:, None] + NA_KW - 1, 0, 2 * NA_KW - 2)
    return jnp.where(inside[None, None], rpb[:, :, dc], -jnp.inf)


def kernel(x, attn_norm, w_in, q_norm, w_uq, kv_norm, w_ukv, w_branch_mla, na_rpb, w_branch_na,
           w_gate, b_gate, w_out, ffn_norm, peer_w_query, peer_subkeys, peer_u, peer_v, final_norm):
    batch, seq, d = x.shape
    t = batch * seq
    depth = w_in.shape[0]
    q_rank, kv_rank = q_norm.shape[1], kv_norm.shape[1]
    mla_heads = w_uq.shape[2] // MLA_QK
    na_heads = na_rpb.shape[1]
    na_w = na_heads * NA_D
    cos, sin = _rope_tables(seq)
    xf = x.reshape(t, d)
    for l in range(depth):
        o1, o2, o3 = q_rank, q_rank + kv_rank, q_rank + kv_rank + ROPE
        w_na = w_in[l][:, o3:].astype(BF16)
        cq_w = -(-q_rank // kv_rank) * kv_rank
        pad = jnp.zeros((d, cq_w - q_rank), F32)
        w_kr = w_in[l][:, o2:o3]
        w_lat = jnp.concatenate([w_in[l][:, :o1], pad, w_in[l][:, o1:o2], w_kr, _rot_cols(w_kr)], axis=1).astype(BF16)
        ckv_blk = cq_w // kv_rank
        kr_blk = (cq_w + kv_rank) // LANE

        wq3 = w_uq[l].reshape(q_rank, mla_heads, MLA_QK)
        wq = jnp.concatenate([wq3[..., :NOPE], wq3[..., NOPE:], _rot_cols(wq3[..., NOPE:])], axis=-1)
        wq = wq.reshape(q_rank, mla_heads * QK_PAD).astype(BF16)
        wkv3 = w_ukv[l].reshape(kv_rank, mla_heads, NOPE + MLA_V)
        wk = wkv3[..., :NOPE].reshape(kv_rank, mla_heads * NOPE).astype(BF16)
        wv = wkv3[..., NOPE:].reshape(kv_rank, mla_heads * MLA_V).astype(BF16)

        h = _rmsnorm_cast(xf, attn_norm[l])
        qkv = _matmul(h, w_na, BF16, 1024, 1024, name="na_proj")
        lat = _matmul(h, w_lat, F32, 512, w_lat.shape[1], name="mla_latent_proj")
        q_cat = _mla_q(lat, q_norm[l], wq, cos, sin, mla_heads, seq)
        k_cat, v_mla = _mla_kv(lat, ckv_blk, kr_blk, kv_norm[l], wk, wv, cos, sin, mla_heads, seq)
        o_a = _mla_attn(q_cat, k_cat, v_mla, batch, seq, mla_heads)
        o_b = _na_attn(qkv, _na_toeplitz(na_rpb[l]), batch, seq, na_heads)
        merged = _gate_merge(h, w_gate[l].astype(BF16), b_gate[l].reshape(1, -1), o_a,
                             w_branch_mla[l].astype(BF16), o_b, w_branch_na[l].astype(BF16))
        xf = _matmul(merged, w_out[l].astype(BF16), F32, 512, 1024, residual=xf, name="out_proj")

        h2, h2t = _rmsnorm_cast(xf, ffn_norm[l], transposed=True)
        ph = peer_subkeys.shape[1]
        sk = peer_subkeys[l].reshape(2 * ph, N_KEYS, -1).astype(BF16)
        st = _peer_scores(h2, peer_w_query[l].astype(BF16), sk)
        e1, e2, tau = _peer_topk(st)
        peer_t = _peer_main(h2t, peer_u[l].astype(BF16), peer_v[l].T.astype(BF16),
                            st.reshape(ph, 2, N_KEYS, t), e1, e2, tau)
        if l + 1 < depth:
            xf = xf + peer_t.T
    return _final(xf, peer_t, final_norm).reshape(batch, seq, d)
```

```python
import functools
import math

import jax
import jax.numpy as jnp
import numpy as np
from jax import lax
from jax.experimental import pallas as pl
from jax.experimental.pallas import tpu as pltpu

F32 = jnp.float32
BF16 = jnp.bfloat16

EPS = 1e-6
ROPE_BASE = 10000.0
GRID_W = 64
NOPE = 128
ROPE = 64
MLA_V = 128
MLA_QK = NOPE + ROPE
QK_PAD = 256
V_PAD = 256
NA_D = 128
NA_KH = 8
NA_KW = 16
NA_ROWS_PER_BLOCK = 4
NA_KEY_ROWS = 12
NA_BLOCKS_PER_STEP = 4
N_KEYS = 128
PEER_TOPK = 16
LANE = 128
LOG2E = math.log2(math.e)
PEER_GATE_LANES = 128
VMEM_LIMIT = 56 * 1024 * 1024

NT_DIMS = (((1,), (1,)), ((), ()))


def _params(*sem):
    return pltpu.CompilerParams(dimension_semantics=sem, vmem_limit_bytes=VMEM_LIMIT)


def _tile(n, pref):
    t = min(n, pref)
    while n % t:
        t //= 2
    return t


def _rms(x, g):
    return x * lax.rsqrt(jnp.mean(x * x, axis=-1, keepdims=True) + EPS) * g


def _rmsnorm_kernel(x_ref, g_ref, o_ref, *ot_ref):
    y = _rms(x_ref[...], g_ref[...])
    o_ref[...] = y.astype(o_ref.dtype)
    for r in ot_ref:
        r[...] = y.T.astype(r.dtype)


def _rmsnorm_cast(x, g, tm=256, transposed=False):
    m, d = x.shape
    tm = _tile(m, tm)
    out_specs = [pl.BlockSpec((tm, d), lambda i: (i, 0))]
    out_shape = [jax.ShapeDtypeStruct((m, d), BF16)]
    if transposed:
        out_specs.append(pl.BlockSpec((d, tm), lambda i: (0, i)))
        out_shape.append(jax.ShapeDtypeStruct((d, m), BF16))
    out = pl.pallas_call(
        _rmsnorm_kernel,
        grid=(m // tm,),
        in_specs=[pl.BlockSpec((tm, d), lambda i: (i, 0)),
                  pl.BlockSpec((1, d), lambda i: (0, 0))],
        out_specs=out_specs,
        out_shape=out_shape,
        compiler_params=_params("parallel"),
        name="rmsnorm_cast",
    )(x, g.reshape(1, d))
    return out if transposed else out[0]


def _mm_kernel(a_ref, w_ref, o_ref):
    o_ref[...] = jnp.dot(a_ref[...], w_ref[...], preferred_element_type=F32).astype(o_ref.dtype)


def _mm_res_kernel(a_ref, w_ref, r_ref, o_ref):
    acc = jnp.dot(a_ref[...], w_ref[...], preferred_element_type=F32)
    o_ref[...] = (r_ref[...] + acc).astype(o_ref.dtype)


def _matmul(a, w, out_dtype, tm, tn, residual=None, name="matmul"):
    m, k = a.shape
    n = w.shape[1]
    tm, tn = _tile(m, tm), _tile(n, tn)
    in_specs = [pl.BlockSpec((tm, k), lambda i, j: (i, 0)),
                pl.BlockSpec((k, tn), lambda i, j: (0, j))]
    args = [a, w]
    body = _mm_kernel
    if residual is not None:
        in_specs.append(pl.BlockSpec((tm, tn), lambda i, j: (i, j)))
        args.append(residual)
        body = _mm_res_kernel
    return pl.pallas_call(
        body,
        grid=(m // tm, n // tn),
        in_specs=in_specs,
        out_specs=pl.BlockSpec((tm, tn), lambda i, j: (i, j)),
        out_shape=jax.ShapeDtypeStruct((m, n), out_dtype),
        compiler_params=_params("parallel", "parallel"),
        name=name,
    )(*args)


def _rope_pair(blk, cos, sin):
    return blk * cos + pltpu.roll(blk, ROPE, 1) * sin


def _mla_q_kernel(c_ref, g_ref, w_ref, cos_ref, sin_ref, o_ref, *, heads, scale):
    cn = _rms(c_ref[...], g_ref[...]).astype(BF16)
    cos, sin = cos_ref[...], sin_ref[...]
    for h in range(heads):
        y = jnp.dot(cn, w_ref[:, h * QK_PAD:(h + 1) * QK_PAD], preferred_element_type=F32) * scale
        o_ref[:, h * QK_PAD:h * QK_PAD + NOPE] = y[:, :NOPE].astype(o_ref.dtype)
        o_ref[:, h * QK_PAD + NOPE:(h + 1) * QK_PAD] = _rope_pair(y[:, NOPE:], cos, sin).astype(o_ref.dtype)


def _mla_q(lat, g, wq, cos, sin, heads, seq, tm=512):
    t = lat.shape[0]
    r = g.shape[0]
    tm = _tile(seq, tm)
    sb = seq // tm
    return pl.pallas_call(
        functools.partial(_mla_q_kernel, heads=heads, scale=MLA_QK ** -0.5 * LOG2E),
        grid=(t // tm,),
        in_specs=[pl.BlockSpec((tm, r), lambda i: (i, 0)),
                  pl.BlockSpec((1, r), lambda i: (0, 0)),
                  pl.BlockSpec((r, heads * QK_PAD), lambda i: (0, 0)),
                  pl.BlockSpec((tm, LANE), lambda i: (i % sb, 0)),
                  pl.BlockSpec((tm, LANE), lambda i: (i % sb, 0))],
        out_specs=pl.BlockSpec((tm, heads * QK_PAD), lambda i: (i, 0)),
        out_shape=jax.ShapeDtypeStruct((t, heads * QK_PAD), BF16),
        compiler_params=_params("parallel"),
        name="mla_q_proj",
    )(lat, g.reshape(1, r), wq, cos, sin)


def _mla_kv_kernel(c_ref, kr_ref, g_ref, wk_ref, wv_ref, cos_ref, sin_ref, k_ref, v_ref, *, heads):
    cn = _rms(c_ref[...], g_ref[...]).astype(BF16)
    kpe = _rope_pair(kr_ref[...], cos_ref[...], sin_ref[...]).astype(k_ref.dtype)
    kn = jnp.dot(cn, wk_ref[...], preferred_element_type=F32).astype(k_ref.dtype)
    v = jnp.dot(cn, wv_ref[...], preferred_element_type=F32).astype(v_ref.dtype)
    lane = lax.broadcasted_iota(jnp.int32, (cn.shape[0], V_PAD - MLA_V), 1)
    ones_col = jnp.where(lane == 0, 1.0, 0.0).astype(v_ref.dtype)
    for h in range(heads):
        k_ref[:, h * QK_PAD:h * QK_PAD + NOPE] = kn[:, h * NOPE:(h + 1) * NOPE]
        k_ref[:, h * QK_PAD + NOPE:(h + 1) * QK_PAD] = kpe
        v_ref[:, h * V_PAD:h * V_PAD + MLA_V] = v[:, h * MLA_V:(h + 1) * MLA_V]
        v_ref[:, h * V_PAD + MLA_V:(h + 1) * V_PAD] = ones_col


def _mla_kv(lat, ckv_blk, kr_blk, g, wk, wv, cos, sin, heads, seq, tm=512):
    t = lat.shape[0]
    r = g.shape[0]
    tm = _tile(seq, tm)
    sb = seq // tm
    return pl.pallas_call(
        functools.partial(_mla_kv_kernel, heads=heads),
        grid=(t // tm,),
        in_specs=[pl.BlockSpec((tm, r), lambda i: (i, ckv_blk)),
                  pl.BlockSpec((tm, LANE), lambda i: (i, kr_blk)),
                  pl.BlockSpec((1, r), lambda i: (0, 0)),
                  pl.BlockSpec((r, heads * NOPE), lambda i: (0, 0)),
                  pl.BlockSpec((r, heads * MLA_V), lambda i: (0, 0)),
                  pl.BlockSpec((tm, LANE), lambda i: (i % sb, 0)),
                  pl.BlockSpec((tm, LANE), lambda i: (i % sb, 0))],
        out_specs=[pl.BlockSpec((tm, heads * QK_PAD), lambda i: (i, 0)),
                   pl.BlockSpec((tm, heads * V_PAD), lambda i: (i, 0))],
        out_shape=[jax.ShapeDtypeStruct((t, heads * QK_PAD), BF16),
                   jax.ShapeDtypeStruct((t, heads * V_PAD), BF16)],
        compiler_params=_params("parallel"),
        name="mla_kv_proj",
    )(lat, lat, g.reshape(1, r), wk, wv, cos, sin)


def _mla_attn_kernel(q_ref, k_ref, v_ref, o_ref, *, tk):
    q = q_ref[...]
    m = acc = None
    for j in range(k_ref.shape[0] // tk):
        s = lax.dot_general(q, k_ref[j * tk:(j + 1) * tk, :], NT_DIMS, preferred_element_type=F32)
        m_blk = jnp.max(s, axis=1, keepdims=True)
        m_new = m_blk if j == 0 else jnp.maximum(m, m_blk)
        p = jnp.exp2(s - m_new).astype(BF16)
        pv = jnp.dot(p, v_ref[j * tk:(j + 1) * tk, :], preferred_element_type=F32)
        acc = pv if j == 0 else jnp.exp2(m - m_new) * acc + pv
        m = m_new
    o_ref[...] = (acc[:, :MLA_V] / acc[:, MLA_V:MLA_V + 1]).astype(o_ref.dtype)


def _mla_attn(q, k, v, batch, seq, heads, tq=1024, tk=1024):
    t = q.shape[0]
    tq, tk = _tile(seq, tq), _tile(seq, tk)
    qb = seq // tq
    return pl.pallas_call(
        functools.partial(_mla_attn_kernel, tk=tk),
        grid=(batch, heads, qb),
        in_specs=[pl.BlockSpec((tq, QK_PAD), lambda b, h, i: (b * qb + i, h)),
                  pl.BlockSpec((seq, QK_PAD), lambda b, h, i: (b, h)),
                  pl.BlockSpec((seq, V_PAD), lambda b, h, i: (b, h))],
        out_specs=pl.BlockSpec((tq, MLA_V), lambda b, h, i: (b * qb + i, h)),
        out_shape=jax.ShapeDtypeStruct((t, heads * MLA_V), BF16),
        compiler_params=_params("parallel", "parallel", "parallel"),
        name="mla_attention",
    )(q, k, v)


def _na_block_layout(rows):
    nb = rows // NA_ROWS_PER_BLOCK
    kh = min(NA_KH, rows)
    classes = []
    for blk in (0, 1, nb - 1):
        r0 = blk * NA_ROWS_PER_BLOCK
        ks = min(max(r0 - NA_KH // 2, 0), rows - NA_KEY_ROWS)
        entries = []
        for rq in range(NA_ROWS_PER_BLOCK):
            r = r0 + rq
            rs = min(max(r - kh // 2, 0), rows - kh)
            for j in range(kh):
                entries.append((rq, rs + j - ks, rs + j - r + NA_KH - 1))
        classes.append(entries)
    return classes


def _na_attn_kernel(q_ref, k_ref, v_ref, toep_ref, o_ref, bias_ref, *, rows, scale):
    b, step = pl.program_id(1), pl.program_id(2)
    nb = rows // NA_ROWS_PER_BLOCK
    tq = NA_ROWS_PER_BLOCK * GRID_W
    nk = NA_KEY_ROWS * GRID_W

    @pl.when((b == 0) & (step == 0))
    def _build_bias():
        bias_ref[...] = jnp.full(bias_ref.shape, -jnp.inf, F32)
        for c, entries in enumerate(_na_block_layout(rows)):
            for rq, rk, d in entries:
                bias_ref[c, rq * GRID_W:(rq + 1) * GRID_W, rk * GRID_W:(rk + 1) * GRID_W] = toep_ref[0, d] * LOG2E

    for sub in range(NA_BLOCKS_PER_STEP):
        blk = step * NA_BLOCKS_PER_STEP + sub
        ks = jnp.clip(blk * NA_ROWS_PER_BLOCK - NA_KH // 2, 0, rows - NA_KEY_ROWS)
        off = pl.multiple_of(ks * GRID_W, GRID_W)
        cls = jnp.where(blk == 0, 0, jnp.where(blk == nb - 1, 2, 1))
        q = q_ref[sub * tq:(sub + 1) * tq, :]
        s = lax.dot_general(q, k_ref[pl.ds(off, nk), :], NT_DIMS, preferred_element_type=F32)
        s = s * scale + bias_ref[cls]
        m = jnp.max(s, axis=1, keepdims=True)
        p = jnp.exp2(s - m)
        l = jnp.sum(p, axis=1, keepdims=True)
        o = jnp.dot(p.astype(BF16), v_ref[pl.ds(off, nk), :], preferred_element_type=F32)
        o_ref[sub * tq:(sub + 1) * tq, :] = (o / l).astype(o_ref.dtype)


def _na_attn(qkv, toep, batch, seq, heads):
    t = qkv.shape[0]
    rows = seq // GRID_W
    tq = NA_ROWS_PER_BLOCK * GRID_W
    tstep = NA_BLOCKS_PER_STEP * tq
    ns = seq // tstep
    return pl.pallas_call(
        functools.partial(_na_attn_kernel, rows=rows, scale=NA_D ** -0.5 * LOG2E),
        grid=(heads, batch, ns),
        in_specs=[pl.BlockSpec((tstep, NA_D), lambda h, b, i: (b * ns + i, h)),
                  pl.BlockSpec((seq, NA_D), lambda h, b, i: (b, heads + h)),
                  pl.BlockSpec((seq, NA_D), lambda h, b, i: (b, 2 * heads + h)),
                  pl.BlockSpec((1, 2 * NA_KH - 1, GRID_W, GRID_W), lambda h, b, i: (h, 0, 0, 0))],
        out_specs=pl.BlockSpec((tstep, NA_D), lambda h, b, i: (b * ns + i, h)),
        out_shape=jax.ShapeDtypeStruct((t, heads * NA_D), BF16),
        scratch_shapes=[pltpu.VMEM((3, tq, NA_KEY_ROWS * GRID_W), F32)],
        compiler_params=_params("arbitrary", "arbitrary", "arbitrary"),
        name="na_attention",
    )(qkv, qkv, qkv, toep)


def _gate_merge_kernel(h_ref, wga_ref, wgb_ref, ba_ref, bb_ref, oa_ref, pa_ref, ob_ref, pb_ref, o_ref):
    h = h_ref[...]
    ga = jax.nn.sigmoid(jnp.dot(h, wga_ref[...], preferred_element_type=F32) + ba_ref[...])
    ya = jnp.dot(oa_ref[...], pa_ref[...], preferred_element_type=F32)
    acc = ga * ya
    gb = jax.nn.sigmoid(jnp.dot(h, wgb_ref[...], preferred_element_type=F32) + bb_ref[...])
    yb = jnp.dot(ob_ref[...], pb_ref[...], preferred_element_type=F32)
    o_ref[...] = (acc + gb * yb).astype(o_ref.dtype)


def _gate_merge(h, w_gate, b_gate, oa, pa, ob, pb, tm=512, tn=512):
    m, d = h.shape
    wa, wb = oa.shape[1], ob.shape[1]
    tm, tn = _tile(m, tm), _tile(d, tn)
    nj = d // tn
    return pl.pallas_call(
        _gate_merge_kernel,
        grid=(m // tm, nj),
        in_specs=[pl.BlockSpec((tm, d), lambda i, j: (i, 0)),
                  pl.BlockSpec((d, tn), lambda i, j: (0, j)),
                  pl.BlockSpec((d, tn), lambda i, j: (0, nj + j)),
                  pl.BlockSpec((1, tn), lambda i, j: (0, j)),
                  pl.BlockSpec((1, tn), lambda i, j: (0, nj + j)),
                  pl.BlockSpec((tm, wa), lambda i, j: (i, 0)),
                  pl.BlockSpec((wa, tn), lambda i, j: (0, j)),
                  pl.BlockSpec((tm, wb), lambda i, j: (i, 0)),
                  pl.BlockSpec((wb, tn), lambda i, j: (0, j))],
        out_specs=pl.BlockSpec((tm, tn), lambda i, j: (i, j)),
        out_shape=jax.ShapeDtypeStruct((m, d), BF16),
        compiler_params=_params("parallel", "parallel"),
        name="gate_merge",
    )(h, w_gate, w_gate, b_gate, b_gate, oa, pa, ob, pb)


def _peer_scores_kernel(x_ref, w_ref, sk_ref, o_ref):
    q = jnp.dot(x_ref[...], w_ref[...], preferred_element_type=F32).astype(BF16)
    for p in range(2):
        o_ref[p] = lax.dot_general(sk_ref[p], q[:, p * LANE:(p + 1) * LANE], NT_DIMS,
                                   preferred_element_type=F32)


def _peer_scores(x, wq, sk, tm=1024):
    t, d = x.shape
    ph = sk.shape[0] // 2
    half = sk.shape[2]
    tm = _tile(t, tm)
    return pl.pallas_call(
        _peer_scores_kernel,
        grid=(t // tm, ph),
        in_specs=[pl.BlockSpec((tm, d), lambda i, h: (i, 0)),
                  pl.BlockSpec((d, 2 * half), lambda i, h: (0, h)),
                  pl.BlockSpec((2, N_KEYS, half), lambda i, h: (h, 0, 0))],
        out_specs=pl.BlockSpec((2, N_KEYS, tm), lambda i, h: (h, 0, i)),
        out_shape=jax.ShapeDtypeStruct((2 * ph, N_KEYS, t), F32),
        compiler_params=_params("parallel", "parallel"),
        name="peer_scores",
    )(x, wq, sk)


def _top_ranked(vals, n, ties):
    rowid = lax.broadcasted_iota(jnp.int32, vals.shape, 0)
    big = jnp.int32(vals.shape[0])
    rank = jnp.full(vals.shape, float(n), F32)
    out = []
    for k in range(n):
        m = jnp.max(vals, axis=0, keepdims=True)
        hit = vals == m
        if ties:
            hit = rowid == jnp.min(jnp.where(hit, rowid, big), axis=0, keepdims=True)
        vals = jnp.where(hit, -jnp.inf, vals)
        rank = jnp.where(hit, float(k), rank)
        out.append(m)
    taken = jnp.sum(jnp.where(rank < float(n), 1.0, 0.0), axis=0, keepdims=True)
    return out, rank, jnp.all(taken == float(n))


def _peer_topk_tables(s1, s2, ties):
    v1, rank1, ok1 = _top_ranked(s1, PEER_TOPK, ties)
    v2, rank2, ok2 = _top_ranked(s2, PEER_TOPK, ties)
    widths = [PEER_TOPK // (a + 1) for a in range(PEER_TOPK)]
    rows = [v1[a] + v2[b] for a in range(PEER_TOPK) for b in range(widths[a])]
    rows += [jnp.full_like(rows[0], -jnp.inf)] * (-len(rows) % 8)
    c, crank, ok3 = _top_ranked(jnp.concatenate(rows, axis=0), PEER_TOPK, ties)
    taken = jnp.where(crank < PEER_TOPK, 1.0, 0.0)
    theta = jnp.zeros_like(s1)
    off = 0
    for a in range(PEER_TOPK):
        b_a = jnp.sum(taken[off:off + widths[a]], axis=0, keepdims=True)
        theta = jnp.where(rank1 == float(a), b_a, theta)
        off += widths[a]
    z = c[0] * 0.0
    for ck in c:
        z = z + jnp.exp(ck - c[0])
    return (theta, rank2, jnp.exp(s1 - v1[0]), jnp.exp(s2 - v2[0]) / z), ok1 & ok2 & ok3


def _peer_topk_kernel(s_ref, e1_ref, e2_ref, theta_ref, rank2_ref):
    def emit(tables):
        theta_ref[0], rank2_ref[0], e1_ref[0], e2_ref[0] = tables

    tables, distinct = _peer_topk_tables(s_ref[0], s_ref[1], ties=False)
    emit(tables)

    @pl.when(jnp.logical_not(distinct))
    def _redo():
        emit(_peer_topk_tables(s_ref[0], s_ref[1], ties=True)[0])


def _peer_topk(st, tt=256):
    g, n, t = st.shape
    ph = g // 2
    tt = _tile(t, tt)
    tab = jax.ShapeDtypeStruct((ph, n, t), F32)
    return pl.pallas_call(
        _peer_topk_kernel,
        grid=(ph, t // tt),
        in_specs=[pl.BlockSpec((2, n, tt), lambda h, i: (h, 0, i))],
        out_specs=[pl.BlockSpec((1, n, tt), lambda h, i: (h, 0, i))] * 4,
        out_shape=[tab] * 4,
        compiler_params=_params("parallel", "parallel"),
        name="peer_topk",
    )(st)


def _peer_main_kernel(xt_ref, u_ref, vt_ref, theta_ref, rank2_ref, e1_ref, e2_ref, o_ref, g_ref, at_ref,
                      *, te, ph, n_e):
    e = pl.program_id(1)
    d, tm = o_ref.shape
    groups = te // N_KEYS

    def gates(c, zero):
        i = e * groups + c
        theta = [theta_ref[h, pl.ds(i, 1), :] + zero for h in range(ph)]
        e1 = [e1_ref[h, pl.ds(i, 1), :] for h in range(ph)]
        gl = min(tm, PEER_GATE_LANES)
        for l0 in range(0, tm, gl):
            lanes = slice(l0, l0 + gl)
            g = None
            for h in range(ph):
                sel = rank2_ref[h, :, lanes] < theta[h][:, lanes]
                w = jnp.where(sel, e1[h][:, lanes] * e2_ref[h, :, lanes], 0.0)
                g = w if g is None else g + w
            g_ref[c * N_KEYS:(c + 1) * N_KEYS, lanes] = g

    def step(first, last):
        if first:
            o_ref[...] = jnp.zeros(o_ref.shape, o_ref.dtype)
        else:
            o_ref[...] += jnp.dot(vt_ref[...], at_ref[...], preferred_element_type=F32)
        if last:
            return
        bits = lax.bitcast_convert_type(o_ref[d - 1:d, :], jnp.int32)
        zero = lax.shift_right_logical(lax.shift_right_logical(bits, 31), 1).astype(F32)
        for c in range(groups):
            gates(c, zero)
        ht = jnp.dot(u_ref[...], xt_ref[...], preferred_element_type=F32)
        act = 0.5 * ht * (1.0 + lax.erf(ht * (2.0 ** -0.5)))
        at_ref[...] = (g_ref[...] * act).astype(at_ref.dtype)

    pl.when(e == 0)(functools.partial(step, True, False))
    pl.when((e > 0) & (e < n_e))(functools.partial(step, False, False))
    pl.when(e == n_e)(functools.partial(step, False, True))


def _peer_main(xt, u, vt, theta, rank2, e1, e2, tm=512, te=512):
    d, t = xt.shape
    n_exp = u.shape[0]
    ph = e1.shape[0]
    tm, te = _tile(t, tm), _tile(n_exp, te)
    n_e = n_exp // te
    once = pl.Buffered(1)
    return pl.pallas_call(
        functools.partial(_peer_main_kernel, te=te, ph=ph, n_e=n_e),
        grid=(t // tm, n_e + 1),
        in_specs=[pl.BlockSpec((d, tm), lambda i, e: (0, i), pipeline_mode=once),
                  pl.BlockSpec((te, d), lambda i, e: (jnp.minimum(e, n_e - 1), 0)),
                  pl.BlockSpec((d, te), lambda i, e: (0, jnp.maximum(e - 1, 0))),
                  pl.BlockSpec((ph, N_KEYS, tm), lambda i, e: (0, 0, i), pipeline_mode=once),
                  pl.BlockSpec((ph, N_KEYS, tm), lambda i, e: (0, 0, i), pipeline_mode=once),
                  pl.BlockSpec((ph, N_KEYS, tm), lambda i, e: (0, 0, i), pipeline_mode=once),
                  pl.BlockSpec((ph, N_KEYS, tm), lambda i, e: (0, 0, i), pipeline_mode=once)],
        out_specs=pl.BlockSpec((d, tm), lambda i, e: (0, i)),
        out_shape=jax.ShapeDtypeStruct((d, t), F32),
        scratch_shapes=[pltpu.VMEM((te, tm), F32), pltpu.VMEM((te, tm), BF16)],
        compiler_params=_params("parallel", "arbitrary"),
        name="peer_experts",
    )(xt, u, vt, theta, rank2, e1, e2)


def _final_kernel(x_ref, pt_ref, g_ref, o_ref):
    o_ref[...] = _rms(x_ref[...] + pt_ref[...].T, g_ref[...])


def _final(x1, pt, g, tm=256):
    t, d = x1.shape
    tm = _tile(t, tm)
    return pl.pallas_call(
        _final_kernel,
        grid=(t // tm,),
        in_specs=[pl.BlockSpec((tm, d), lambda i: (i, 0)),
                  pl.BlockSpec((d, tm), lambda i: (0, i)),
                  pl.BlockSpec((1, d), lambda i: (0, 0))],
        out_specs=pl.BlockSpec((tm, d), lambda i: (i, 0)),
        out_shape=jax.ShapeDtypeStruct((t, d), F32),
        compiler_params=_params("parallel"),
        name="final_norm",
    )(x1, pt, g.reshape(1, d))


def _rot_cols(w):
    q = ROPE // 4
    idx = np.concatenate([np.arange(q, 2 * q), np.arange(0, q), np.arange(3 * q, 4 * q), np.arange(2 * q, 3 * q)])
    sign = np.concatenate([-np.ones(q), np.ones(q), -np.ones(q), np.ones(q)]).astype(np.float32)
    return w[..., idx] * sign


def _rope_tables(seq):
    t = jnp.arange(seq)
    half = ROPE // 4
    freqs = ROPE_BASE ** (-jnp.arange(half, dtype=F32) / half)
    ang_r = (t // GRID_W).astype(F32)[:, None] * freqs[None, :]
    ang_c = (t % GRID_W).astype(F32)[:, None] * freqs[None, :]
    zeros = jnp.zeros((seq, LANE - ROPE), F32)
    cos = jnp.concatenate([jnp.cos(ang_r), jnp.cos(ang_r), jnp.cos(ang_c), jnp.cos(ang_c), zeros], axis=1)
    sin = jnp.concatenate([jnp.sin(ang_r), jnp.sin(ang_r), jnp.sin(ang_c), jnp.sin(ang_c), zeros], axis=1)
    return cos, sin


def _na_toeplitz(rpb):
    cols = np.arange(GRID_W)
    start = np.clip(cols - NA_KW // 2, 0, GRID_W - NA_KW)
    ck = cols[None, :]
    inside = (ck >= start[:, None]) & (ck < start[:, None] + NA_KW)
    dc = np.clip(ck - cols[:, None] + NA_KW - 1, 0, 2 * NA_KW - 2)
    return jnp.where(inside[None, None], rpb[:, :, dc], -jnp.inf)


def kernel(x, attn_norm, w_in, q_norm, w_uq, kv_norm, w_ukv, w_branch_mla, na_rpb, w_branch_na,
           w_gate, b_gate, w_out, ffn_norm, peer_w_query, peer_subkeys, peer_u, peer_v, final_norm):
    batch, seq, d = x.shape
    t = batch * seq
    depth = w_in.shape[0]
    q_rank, kv_rank = q_norm.shape[1], kv_norm.shape[1]
    mla_heads = w_uq.shape[2] // MLA_QK
    na_heads = na_rpb.shape[1]
    na_w = na_heads * NA_D
    cos, sin = _rope_tables(seq)
    xf = x.reshape(t, d)
    for l in range(depth):
        o1, o2, o3 = q_rank, q_rank + kv_rank, q_rank + kv_rank + ROPE
        w_na = w_in[l][:, o3:].astype(BF16)
        cq_w = -(-q_rank // kv_rank) * kv_rank
        pad = jnp.zeros((d, cq_w - q_rank), F32)
        w_kr = w_in[l][:, o2:o3]
        w_lat = jnp.concatenate([w_in[l][:, :o1], pad, w_in[l][:, o1:o2], w_kr, _rot_cols(w_kr)], axis=1).astype(BF16)
        ckv_blk = cq_w // kv_rank
        kr_blk = (cq_w + kv_rank) // LANE

        wq3 = w_uq[l].reshape(q_rank, mla_heads, MLA_QK)
        wq = jnp.concatenate([wq3[..., :NOPE], wq3[..., NOPE:], _rot_cols(wq3[..., NOPE:])], axis=-1)
        wq = wq.reshape(q_rank, mla_heads * QK_PAD).astype(BF16)
        wkv3 = w_ukv[l].reshape(kv_rank, mla_heads, NOPE + MLA_V)
        wk = wkv3[..., :NOPE].reshape(kv_rank, mla_heads * NOPE).astype(BF16)
        wv = wkv3[..., NOPE:].reshape(kv_rank, mla_heads * MLA_V).astype(BF16)

        h = _rmsnorm_cast(xf, attn_norm[l])
        qkv = _matmul(h, w_na, BF16, 1024, 1024, name="na_proj")
        lat = _matmul(h, w_lat, F32, 512, w_lat.shape[1], name="mla_latent_proj")
        q_cat = _mla_q(lat, q_norm[l], wq, cos, sin, mla_heads, seq)
        k_cat, v_mla = _mla_kv(lat, ckv_blk, kr_blk, kv_norm[l], wk, wv, cos, sin, mla_heads, seq)
        o_a = _mla_attn(q_cat, k_cat, v_mla, batch, seq, mla_heads)
        o_b = _na_attn(qkv, _na_toeplitz(na_rpb[l]), batch, seq, na_heads)
        merged = _gate_merge(h, w_gate[l].astype(BF16), b_gate[l].reshape(1, -1), o_a,
                             w_branch_mla[l].astype(BF16), o_b, w_branch_na[l].astype(BF16))
        xf = _matmul(merged, w_out[l].astype(BF16), F32, 512, 1024, residual=xf, name="out_proj")

        h2, h2t = _rmsnorm_cast(xf, ffn_norm[l], transposed=True)
        ph = peer_subkeys.shape[1]
        sk = peer_subkeys[l].reshape(2 * ph, N_KEYS, -1).astype(BF16)
        st = _peer_scores(h2, peer_w_query[l].astype(BF16), sk)
        e1, e2, theta, rank2 = _peer_topk(st)
        peer_t = _peer_main(h2t, peer_u[l].astype(BF16), peer_v[l].T.astype(BF16), theta, rank2, e1, e2)
        if l + 1 < depth:
            xf = xf + peer_t.T
    return _final(xf, peer_t, final_norm).reshape(batch, seq, d)
```

```python
import functools
import math

import jax
import jax.numpy as jnp
import numpy as np
from jax import lax
from jax.experimental import pallas as pl
from jax.experimental.pallas import tpu as pltpu

F32 = jnp.float32
BF16 = jnp.bfloat16

EPS = 1e-6
ROPE_BASE = 10000.0
GRID_W = 64
NOPE = 128
ROPE = 64
MLA_V = 128
MLA_QK = NOPE + ROPE
QK_PAD = 256
V_PAD = 256
NA_D = 128
NA_KH = 8
NA_KW = 16
NA_ROWS_PER_BLOCK = 4
NA_KEY_ROWS = 12
NA_BLOCKS_PER_STEP = 4
N_KEYS = 128
PEER_TOPK = 16
LANE = 128
LOG2E = math.log2(math.e)
PEER_GATE_LANES = 128
VMEM_LIMIT = 56 * 1024 * 1024

NT_DIMS = (((1,), (1,)), ((), ()))


def _params(*sem):
    return pltpu.CompilerParams(dimension_semantics=sem, vmem_limit_bytes=VMEM_LIMIT)


def _tile(n, pref):
    t = min(n, pref)
    while n % t:
        t //= 2
    return t


def _rms(x, g):
    return x * lax.rsqrt(jnp.mean(x * x, axis=-1, keepdims=True) + EPS) * g


def _rmsnorm_kernel(x_ref, g_ref, o_ref, *ot_ref):
    y = _rms(x_ref[...], g_ref[...])
    o_ref[...] = y.astype(o_ref.dtype)
    for r in ot_ref:
        r[...] = y.T.astype(r.dtype)


def _rmsnorm_cast(x, g, tm=256, transposed=False):
    m, d = x.shape
    tm = _tile(m, tm)
    out_specs = [pl.BlockSpec((tm, d), lambda i: (i, 0))]
    out_shape = [jax.ShapeDtypeStruct((m, d), BF16)]
    if transposed:
        out_specs.append(pl.BlockSpec((d, tm), lambda i: (0, i)))
        out_shape.append(jax.ShapeDtypeStruct((d, m), BF16))
    out = pl.pallas_call(
        _rmsnorm_kernel,
        grid=(m // tm,),
        in_specs=[pl.BlockSpec((tm, d), lambda i: (i, 0)),
                  pl.BlockSpec((1, d), lambda i: (0, 0))],
        out_specs=out_specs,
        out_shape=out_shape,
        compiler_params=_params("parallel"),
        name="rmsnorm_cast",
    )(x, g.reshape(1, d))
    return out if transposed else out[0]


def _mm_kernel(a_ref, w_ref, o_ref):
    o_ref[...] = jnp.dot(a_ref[...], w_ref[...], preferred_element_type=F32).astype(o_ref.dtype)


def _mm_res_kernel(a_ref, w_ref, r_ref, o_ref):
    acc = jnp.dot(a_ref[...], w_ref[...], preferred_element_type=F32)
    o_ref[...] = (r_ref[...] + acc).astype(o_ref.dtype)


def _matmul(a, w, out_dtype, tm, tn, residual=None, name="matmul"):
    m, k = a.shape
    n = w.shape[1]
    tm, tn = _tile(m, tm), _tile(n, tn)
    in_specs = [pl.BlockSpec((tm, k), lambda i, j: (i, 0)),
                pl.BlockSpec((k, tn), lambda i, j: (0, j))]
    args = [a, w]
    body = _mm_kernel
    if residual is not None:
        in_specs.append(pl.BlockSpec((tm, tn), lambda i, j: (i, j)))
        args.append(residual)
        body = _mm_res_kernel
    return pl.pallas_call(
        body,
        grid=(m // tm, n // tn),
        in_specs=in_specs,
        out_specs=pl.BlockSpec((tm, tn), lambda i, j: (i, j)),
        out_shape=jax.ShapeDtypeStruct((m, n), out_dtype),
        compiler_params=_params("parallel", "parallel"),
        name=name,
    )(*args)


def _rope_pair(blk, cos, sin):
    return blk * cos + pltpu.roll(blk, ROPE, 1) * sin


def _mla_q_kernel(c_ref, g_ref, w_ref, cos_ref, sin_ref, o_ref, *, heads, scale):
    cn = _rms(c_ref[...], g_ref[...]).astype(BF16)
    cos, sin = cos_ref[...], sin_ref[...]
    for h in range(heads):
        y = jnp.dot(cn, w_ref[:, h * QK_PAD:(h + 1) * QK_PAD], preferred_element_type=F32) * scale
        o_ref[:, h * QK_PAD:h * QK_PAD + NOPE] = y[:, :NOPE].astype(o_ref.dtype)
        o_ref[:, h * QK_PAD + NOPE:(h + 1) * QK_PAD] = _rope_pair(y[:, NOPE:], cos, sin).astype(o_ref.dtype)


def _mla_q(lat, g, wq, cos, sin, heads, seq, tm=512):
    t = lat.shape[0]
    r = g.shape[0]
    tm = _tile(seq, tm)
    sb = seq // tm
    return pl.pallas_call(
        functools.partial(_mla_q_kernel, heads=heads, scale=MLA_QK ** -0.5 * LOG2E),
        grid=(t // tm,),
        in_specs=[pl.BlockSpec((tm, r), lambda i: (i, 0)),
                  pl.BlockSpec((1, r), lambda i: (0, 0)),
                  pl.BlockSpec((r, heads * QK_PAD), lambda i: (0, 0)),
                  pl.BlockSpec((tm, LANE), lambda i: (i % sb, 0)),
                  pl.BlockSpec((tm, LANE), lambda i: (i % sb, 0))],
        out_specs=pl.BlockSpec((tm, heads * QK_PAD), lambda i: (i, 0)),
        out_shape=jax.ShapeDtypeStruct((t, heads * QK_PAD), BF16),
        compiler_params=_params("parallel"),
        name="mla_q_proj",
    )(lat, g.reshape(1, r), wq, cos, sin)


def _mla_kv_kernel(c_ref, kr_ref, g_ref, wk_ref, wv_ref, cos_ref, sin_ref, k_ref, v_ref, *, heads):
    cn = _rms(c_ref[...], g_ref[...]).astype(BF16)
    kpe = _rope_pair(kr_ref[...], cos_ref[...], sin_ref[...]).astype(k_ref.dtype)
    kn = jnp.dot(cn, wk_ref[...], preferred_element_type=F32).astype(k_ref.dtype)
    v = jnp.dot(cn, wv_ref[...], preferred_element_type=F32).astype(v_ref.dtype)
    lane = lax.broadcasted_iota(jnp.int32, (cn.shape[0], V_PAD - MLA_V), 1)
    ones_col = jnp.where(lane == 0, 1.0, 0.0).astype(v_ref.dtype)
    for h in range(heads):
        k_ref[:, h * QK_PAD:h * QK_PAD + NOPE] = kn[:, h * NOPE:(h + 1) * NOPE]
        k_ref[:, h * QK_PAD + NOPE:(h + 1) * QK_PAD] = kpe
        v_ref[:, h * V_PAD:h * V_PAD + MLA_V] = v[:, h * MLA_V:(h + 1) * MLA_V]
        v_ref[:, h * V_PAD + MLA_V:(h + 1) * V_PAD] = ones_col


def _mla_kv(lat, ckv_blk, kr_blk, g, wk, wv, cos, sin, heads, seq, tm=512):
    t = lat.shape[0]
    r = g.shape[0]
    tm = _tile(seq, tm)
    sb = seq // tm
    return pl.pallas_call(
        functools.partial(_mla_kv_kernel, heads=heads),
        grid=(t // tm,),
        in_specs=[pl.BlockSpec((tm, r), lambda i: (i, ckv_blk)),
                  pl.BlockSpec((tm, LANE), lambda i: (i, kr_blk)),
                  pl.BlockSpec((1, r), lambda i: (0, 0)),
                  pl.BlockSpec((r, heads * NOPE), lambda i: (0, 0)),
                  pl.BlockSpec((r, heads * MLA_V), lambda i: (0, 0)),
                  pl.BlockSpec((tm, LANE), lambda i: (i % sb, 0)),
                  pl.BlockSpec((tm, LANE), lambda i: (i % sb, 0))],
        out_specs=[pl.BlockSpec((tm, heads * QK_PAD), lambda i: (i, 0)),
                   pl.BlockSpec((tm, heads * V_PAD), lambda i: (i, 0))],
        out_shape=[jax.ShapeDtypeStruct((t, heads * QK_PAD), BF16),
                   jax.ShapeDtypeStruct((t, heads * V_PAD), BF16)],
        compiler_params=_params("parallel"),
        name="mla_kv_proj",
    )(lat, lat, g.reshape(1, r), wk, wv, cos, sin)


def _mla_attn_kernel(q_ref, k_ref, v_ref, o_ref, *, tk):
    q = q_ref[...]
    m = acc = None
    for j in range(k_ref.shape[0] // tk):
        s = lax.dot_general(q, k_ref[j * tk:(j + 1) * tk, :], NT_DIMS, preferred_element_type=F32)
        m_blk = jnp.max(s, axis=1, keepdims=True)
        m_new = m_blk if j == 0 else jnp.maximum(m, m_blk)
        p = jnp.exp2(s - m_new).astype(BF16)
        pv = jnp.dot(p, v_ref[j * tk:(j + 1) * tk, :], preferred_element_type=F32)
        acc = pv if j == 0 else jnp.exp2(m - m_new) * acc + pv
        m = m_new
    o_ref[...] = (acc[:, :MLA_V] / acc[:, MLA_V:MLA_V + 1]).astype(o_ref.dtype)


def _mla_attn(q, k, v, batch, seq, heads, tq=1024, tk=1024):
    t = q.shape[0]
    tq, tk = _tile(seq, tq), _tile(seq, tk)
    qb = seq // tq
    return pl.pallas_call(
        functools.partial(_mla_attn_kernel, tk=tk),
        grid=(batch, heads, qb),
        in_specs=[pl.BlockSpec((tq, QK_PAD), lambda b, h, i: (b * qb + i, h)),
                  pl.BlockSpec((seq, QK_PAD), lambda b, h, i: (b, h)),
                  pl.BlockSpec((seq, V_PAD), lambda b, h, i: (b, h))],
        out_specs=pl.BlockSpec((tq, MLA_V), lambda b, h, i: (b * qb + i, h)),
        out_shape=jax.ShapeDtypeStruct((t, heads * MLA_V), BF16),
        compiler_params=_params("parallel", "parallel", "parallel"),
        name="mla_attention",
    )(q, k, v)


def _na_block_layout(rows):
    nb = rows // NA_ROWS_PER_BLOCK
    kh = min(NA_KH, rows)
    classes = []
    for blk in (0, 1, nb - 1):
        r0 = blk * NA_ROWS_PER_BLOCK
        ks = min(max(r0 - NA_KH // 2, 0), rows - NA_KEY_ROWS)
        entries = []
        for rq in range(NA_ROWS_PER_BLOCK):
            r = r0 + rq
            rs = min(max(r - kh // 2, 0), rows - kh)
            for j in range(kh):
                entries.append((rq, rs + j - ks, rs + j - r + NA_KH - 1))
        classes.append(entries)
    return classes


def _na_attn_kernel(q_ref, k_ref, v_ref, toep_ref, o_ref, bias_ref, *, rows, scale):
    b, step = pl.program_id(1), pl.program_id(2)
    nb = rows // NA_ROWS_PER_BLOCK
    tq = NA_ROWS_PER_BLOCK * GRID_W
    nk = NA_KEY_ROWS * GRID_W

    @pl.when((b == 0) & (step == 0))
    def _build_bias():
        bias_ref[...] = jnp.full(bias_ref.shape, -jnp.inf, F32)
        for c, entries in enumerate(_na_block_layout(rows)):
            for rq, rk, d in entries:
                bias_ref[c, rq * GRID_W:(rq + 1) * GRID_W, rk * GRID_W:(rk + 1) * GRID_W] = toep_ref[0, d] * LOG2E

    for sub in range(NA_BLOCKS_PER_STEP):
        blk = step * NA_BLOCKS_PER_STEP + sub
        ks = jnp.clip(blk * NA_ROWS_PER_BLOCK - NA_KH // 2, 0, rows - NA_KEY_ROWS)
        off = pl.multiple_of(ks * GRID_W, GRID_W)
        cls = jnp.where(blk == 0, 0, jnp.where(blk == nb - 1, 2, 1))
        q = q_ref[sub * tq:(sub + 1) * tq, :]
        s = lax.dot_general(q, k_ref[pl.ds(off, nk), :], NT_DIMS, preferred_element_type=F32)
        s = s * scale + bias_ref[cls]
        m = jnp.max(s, axis=1, keepdims=True)
        p = jnp.exp2(s - m)
        l = jnp.sum(p, axis=1, keepdims=True)
        o = jnp.dot(p.astype(BF16), v_ref[pl.ds(off, nk), :], preferred_element_type=F32)
        o_ref[sub * tq:(sub + 1) * tq, :] = (o / l).astype(o_ref.dtype)


def _na_attn(qkv, toep, batch, seq, heads):
    t = qkv.shape[0]
    rows = seq // GRID_W
    tq = NA_ROWS_PER_BLOCK * GRID_W
    tstep = NA_BLOCKS_PER_STEP * tq
    ns = seq // tstep
    return pl.pallas_call(
        functools.partial(_na_attn_kernel, rows=rows, scale=NA_D ** -0.5 * LOG2E),
        grid=(heads, batch, ns),
        in_specs=[pl.BlockSpec((tstep, NA_D), lambda h, b, i: (b * ns + i, h)),
                  pl.BlockSpec((seq, NA_D), lambda h, b, i: (b, heads + h)),
                  pl.BlockSpec((seq, NA_D), lambda h, b, i: (b, 2 * heads + h)),
                  pl.BlockSpec((1, 2 * NA_KH - 1, GRID_W, GRID_W), lambda h, b, i: (h, 0, 0, 0))],
        out_specs=pl.BlockSpec((tstep, NA_D), lambda h, b, i: (b * ns + i, h)),
        out_shape=jax.ShapeDtypeStruct((t, heads * NA_D), BF16),
        scratch_shapes=[pltpu.VMEM((3, tq, NA_KEY_ROWS * GRID_W), F32)],
        compiler_params=_params("arbitrary", "arbitrary", "arbitrary"),
        name="na_attention",
    )(qkv, qkv, qkv, toep)


def _gate_merge_kernel(h_ref, wga_ref, wgb_ref, ba_ref, bb_ref, oa_ref, pa_ref, ob_ref, pb_ref, o_ref):
    h = h_ref[...]
    ga = jax.nn.sigmoid(jnp.dot(h, wga_ref[...], preferred_element_type=F32) + ba_ref[...])
    ya = jnp.dot(oa_ref[...], pa_ref[...], preferred_element_type=F32)
    acc = ga * ya
    gb = jax.nn.sigmoid(jnp.dot(h, wgb_ref[...], preferred_element_type=F32) + bb_ref[...])
    yb = jnp.dot(ob_ref[...], pb_ref[...], preferred_element_type=F32)
    o_ref[...] = (acc + gb * yb).astype(o_ref.dtype)


def _gate_merge(h, w_gate, b_gate, oa, pa, ob, pb, tm=512, tn=512):
    m, d = h.shape
    wa, wb = oa.shape[1], ob.shape[1]
    tm, tn = _tile(m, tm), _tile(d, tn)
    nj = d // tn
    return pl.pallas_call(
        _gate_merge_kernel,
        grid=(m // tm, nj),
        in_specs=[pl.BlockSpec((tm, d), lambda i, j: (i, 0)),
                  pl.BlockSpec((d, tn), lambda i, j: (0, j)),
                  pl.BlockSpec((d, tn), lambda i, j: (0, nj + j)),
                  pl.BlockSpec((1, tn), lambda i, j: (0, j)),
                  pl.BlockSpec((1, tn), lambda i, j: (0, nj + j)),
                  pl.BlockSpec((tm, wa), lambda i, j: (i, 0)),
                  pl.BlockSpec((wa, tn), lambda i, j: (0, j)),
                  pl.BlockSpec((tm, wb), lambda i, j: (i, 0)),
                  pl.BlockSpec((wb, tn), lambda i, j: (0, j))],
        out_specs=pl.BlockSpec((tm, tn), lambda i, j: (i, j)),
        out_shape=jax.ShapeDtypeStruct((m, d), BF16),
        compiler_params=_params("parallel", "parallel"),
        name="gate_merge",
    )(h, w_gate, w_gate, b_gate, b_gate, oa, pa, ob, pb)


def _peer_scores_kernel(x_ref, w_ref, sk_ref, o_ref):
    q = jnp.dot(x_ref[...], w_ref[...], preferred_element_type=F32).astype(BF16)
    for p in range(2):
        o_ref[p] = lax.dot_general(sk_ref[p], q[:, p * LANE:(p + 1) * LANE], NT_DIMS,
                                   preferred_element_type=F32)


def _peer_scores(x, wq, sk, tm=1024):
    t, d = x.shape
    ph = sk.shape[0] // 2
    half = sk.shape[2]
    tm = _tile(t, tm)
    return pl.pallas_call(
        _peer_scores_kernel,
        grid=(t // tm, ph),
        in_specs=[pl.BlockSpec((tm, d), lambda i, h: (i, 0)),
                  pl.BlockSpec((d, 2 * half), lambda i, h: (0, h)),
                  pl.BlockSpec((2, N_KEYS, half), lambda i, h: (h, 0, 0))],
        out_specs=pl.BlockSpec((2, N_KEYS, tm), lambda i, h: (h, 0, i)),
        out_shape=jax.ShapeDtypeStruct((2 * ph, N_KEYS, t), F32),
        compiler_params=_params("parallel", "parallel"),
        name="peer_scores",
    )(x, wq, sk)


def _top_ranked(vals, n, ties):
    rowid = lax.broadcasted_iota(jnp.int32, vals.shape, 0)
    big = jnp.int32(vals.shape[0])
    rank = jnp.full(vals.shape, float(n), F32)
    out = []
    for k in range(n):
        m = jnp.max(vals, axis=0, keepdims=True)
        hit = vals == m
        if ties:
            hit = rowid == jnp.min(jnp.where(hit, rowid, big), axis=0, keepdims=True)
        vals = jnp.where(hit, -jnp.inf, vals)
        rank = jnp.where(hit, float(k), rank)
        out.append(m)
    taken = jnp.sum(jnp.where(rank < float(n), 1.0, 0.0), axis=0, keepdims=True)
    return out, rank, jnp.all(taken == float(n))


def _peer_topk_tables(s1, s2, ties):
    v1, rank1, ok1 = _top_ranked(s1, PEER_TOPK, ties)
    v2, rank2, ok2 = _top_ranked(s2, PEER_TOPK, ties)
    widths = [PEER_TOPK // (a + 1) for a in range(PEER_TOPK)]
    rows = [v1[a] + v2[b] for a in range(PEER_TOPK) for b in range(widths[a])]
    rows += [jnp.full_like(rows[0], -jnp.inf)] * (-len(rows) % 8)
    c, crank, ok3 = _top_ranked(jnp.concatenate(rows, axis=0), PEER_TOPK, ties)
    taken = jnp.where(crank < PEER_TOPK, 1.0, 0.0)
    theta = jnp.zeros_like(s1)
    off = 0
    for a in range(PEER_TOPK):
        b_a = jnp.sum(taken[off:off + widths[a]], axis=0, keepdims=True)
        theta = jnp.where(rank1 == float(a), b_a, theta)
        off += widths[a]
    z = c[0] * 0.0
    for ck in c:
        z = z + jnp.exp(ck - c[0])
    return (theta, rank2, jnp.exp(s1 - v1[0]), jnp.exp(s2 - v2[0]) / z), ok1 & ok2 & ok3


def _peer_topk_kernel(s_ref, e1_ref, e2_ref, theta_ref, rank2_ref):
    def emit(tables):
        theta_ref[0], rank2_ref[0], e1_ref[0], e2_ref[0] = tables

    tables, distinct = _peer_topk_tables(s_ref[0], s_ref[1], ties=False)
    emit(tables)

    @pl.when(jnp.logical_not(distinct))
    def _redo():
        emit(_peer_topk_tables(s_ref[0], s_ref[1], ties=True)[0])


def _peer_topk(st, tt=256):
    g, n, t = st.shape
    ph = g // 2
    tt = _tile(t, tt)
    tab = jax.ShapeDtypeStruct((ph, n, t), F32)
    return pl.pallas_call(
        _peer_topk_kernel,
        grid=(ph, t // tt),
        in_specs=[pl.BlockSpec((2, n, tt), lambda h, i: (h, 0, i))],
        out_specs=[pl.BlockSpec((1, n, tt), lambda h, i: (h, 0, i))] * 4,
        out_shape=[tab] * 4,
        compiler_params=_params("parallel", "parallel"),
        name="peer_topk",
    )(st)


def _peer_main_kernel(xt_ref, u_ref, vt_ref, theta_ref, rank2_ref, e1_ref, e2_ref, o_ref, g_ref, at_ref,
                      *, te, ph, n_e):
    e = pl.program_id(1)
    d, tm = o_ref.shape
    groups = te // N_KEYS

    def gates(c, zero):
        i = e * groups + c
        theta = [theta_ref[h, pl.ds(i, 1), :] + zero for h in range(ph)]
        e1 = [e1_ref[h, pl.ds(i, 1), :] for h in range(ph)]
        gl = min(tm, PEER_GATE_LANES)
        for l0 in range(0, tm, gl):
            lanes = slice(l0, l0 + gl)
            g = None
            for h in range(ph):
                sel = rank2_ref[h, :, lanes] < theta[h][:, lanes]
                w = jnp.where(sel, e1[h][:, lanes] * e2_ref[h, :, lanes], 0.0)
                g = w if g is None else g + w
            g_ref[c * N_KEYS:(c + 1) * N_KEYS, lanes] = g

    def step(first, last):
        if first:
            o_ref[...] = jnp.zeros(o_ref.shape, o_ref.dtype)
        else:
            o_ref[...] += jnp.dot(vt_ref[...], at_ref[...], preferred_element_type=F32)
        if last:
            return
        bits = lax.bitcast_convert_type(o_ref[d - 1:d, :], jnp.int32)
        zero = lax.shift_right_logical(lax.shift_right_logical(bits, 31), 1).astype(F32)
        for c in range(groups):
            gates(c, zero)
        ht = jnp.dot(u_ref[...], xt_ref[...], preferred_element_type=F32)
        act = 0.5 * ht * (1.0 + lax.erf(ht * (2.0 ** -0.5)))
        at_ref[...] = (g_ref[...] * act).astype(at_ref.dtype)

    pl.when(e == 0)(functools.partial(step, True, False))
    pl.when((e > 0) & (e < n_e))(functools.partial(step, False, False))
    pl.when(e == n_e)(functools.partial(step, False, True))


def _peer_main(xt, u, v, theta, rank2, e1, e2, tm=512, te=512):
    d, t = xt.shape
    n_exp = u.shape[0]
    ph = e1.shape[0]
    tm, te = _tile(t, tm), _tile(n_exp, te)
    n_e = n_exp // te
    u = u.astype(BF16)
    vt = v.reshape(n_e, te, d).swapaxes(1, 2).astype(BF16)
    once = pl.Buffered(1)
    return pl.pallas_call(
        functools.partial(_peer_main_kernel, te=te, ph=ph, n_e=n_e),
        grid=(t // tm, n_e + 1),
        in_specs=[pl.BlockSpec((d, tm), lambda i, e: (0, i), pipeline_mode=once),
                  pl.BlockSpec((te, d), lambda i, e: (jnp.minimum(e, n_e - 1), 0)),
                  pl.BlockSpec((None, d, te), lambda i, e: (jnp.maximum(e - 1, 0), 0, 0)),
                  pl.BlockSpec((ph, N_KEYS, tm), lambda i, e: (0, 0, i), pipeline_mode=once),
                  pl.BlockSpec((ph, N_KEYS, tm), lambda i, e: (0, 0, i), pipeline_mode=once),
                  pl.BlockSpec((ph, N_KEYS, tm), lambda i, e: (0, 0, i), pipeline_mode=once),
                  pl.BlockSpec((ph, N_KEYS, tm), lambda i, e: (0, 0, i), pipeline_mode=once)],
        out_specs=pl.BlockSpec((d, tm), lambda i, e: (0, i)),
        out_shape=jax.ShapeDtypeStruct((d, t), F32),
        scratch_shapes=[pltpu.VMEM((te, tm), F32), pltpu.VMEM((te, tm), BF16)],
        compiler_params=_params("parallel", "arbitrary"),
        name="peer_experts",
    )(xt, u, vt, theta, rank2, e1, e2)


def _final_kernel(x_ref, pt_ref, g_ref, o_ref):
    o_ref[...] = _rms(x_ref[...] + pt_ref[...].T, g_ref[...])


def _final(x1, pt, g, tm=256):
    t, d = x1.shape
    tm = _tile(t, tm)
    return pl.pallas_call(
        _final_kernel,
        grid=(t // tm,),
        in_specs=[pl.BlockSpec((tm, d), lambda i: (i, 0)),
                  pl.BlockSpec((d, tm), lambda i: (0, i)),
                  pl.BlockSpec((1, d), lambda i: (0, 0))],
        out_specs=pl.BlockSpec((tm, d), lambda i: (i, 0)),
        out_shape=jax.ShapeDtypeStruct((t, d), F32),
        compiler_params=_params("parallel"),
        name="final_norm",
    )(x1, pt, g.reshape(1, d))


def _rot_cols(w):
    q = ROPE // 4
    idx = np.concatenate([np.arange(q, 2 * q), np.arange(0, q), np.arange(3 * q, 4 * q), np.arange(2 * q, 3 * q)])
    sign = np.concatenate([-np.ones(q), np.ones(q), -np.ones(q), np.ones(q)]).astype(np.float32)
    return w[..., idx] * sign


def _rope_tables(seq):
    t = jnp.arange(seq)
    half = ROPE // 4
    freqs = ROPE_BASE ** (-jnp.arange(half, dtype=F32) / half)
    ang_r = (t // GRID_W).astype(F32)[:, None] * freqs[None, :]
    ang_c = (t % GRID_W).astype(F32)[:, None] * freqs[None, :]
    zeros = jnp.zeros((seq, LANE - ROPE), F32)
    cos = jnp.concatenate([jnp.cos(ang_r), jnp.cos(ang_r), jnp.cos(ang_c), jnp.cos(ang_c), zeros], axis=1)
    sin = jnp.concatenate([jnp.sin(ang_r), jnp.sin(ang_r), jnp.sin(ang_c), jnp.sin(ang_c), zeros], axis=1)
    return cos, sin


def _na_toeplitz(rpb):
    cols = np.arange(GRID_W)
    start = np.clip(cols - NA_KW // 2, 0, GRID_W - NA_KW)
    ck = cols[None, :]
    inside = (ck >= start[:, None]) & (ck < start[:, None] + NA_KW)
    dc = np.clip(ck - cols[:, None] + NA_KW - 1, 0, 2 * NA_KW - 2)
    return jnp.where(inside[None, None], rpb[:, :, dc], -jnp.inf)


def kernel(x, attn_norm, w_in, q_norm, w_uq, kv_norm, w_ukv, w_branch_mla, na_rpb, w_branch_na,
           w_gate, b_gate, w_out, ffn_norm, peer_w_query, peer_subkeys, peer_u, peer_v, final_norm):
    batch, seq, d = x.shape
    t = batch * seq
    depth = w_in.shape[0]
    q_rank, kv_rank = q_norm.shape[1], kv_norm.shape[1]
    mla_heads = w_uq.shape[2] // MLA_QK
    na_heads = na_rpb.shape[1]
    na_w = na_heads * NA_D
    cos, sin = _rope_tables(seq)
    xf = x.reshape(t, d)
    for l in range(depth):
        o1, o2, o3 = q_rank, q_rank + kv_rank, q_rank + kv_rank + ROPE
        w_na = w_in[l][:, o3:].astype(BF16)
        cq_w = -(-q_rank // kv_rank) * kv_rank
        pad = jnp.zeros((d, cq_w - q_rank), F32)
        w_kr = w_in[l][:, o2:o3]
        w_lat = jnp.concatenate([w_in[l][:, :o1], pad, w_in[l][:, o1:o2], w_kr, _rot_cols(w_kr)], axis=1).astype(BF16)
        ckv_blk = cq_w // kv_rank
        kr_blk = (cq_w + kv_rank) // LANE

        wq3 = w_uq[l].reshape(q_rank, mla_heads, MLA_QK)
        wq = jnp.concatenate([wq3[..., :NOPE], wq3[..., NOPE:], _rot_cols(wq3[..., NOPE:])], axis=-1)
        wq = wq.reshape(q_rank, mla_heads * QK_PAD).astype(BF16)
        wkv3 = w_ukv[l].reshape(kv_rank, mla_heads, NOPE + MLA_V)
        wk = wkv3[..., :NOPE].reshape(kv_rank, mla_heads * NOPE).astype(BF16)
        wv = wkv3[..., NOPE:].reshape(kv_rank, mla_heads * MLA_V).astype(BF16)

        h = _rmsnorm_cast(xf, attn_norm[l])
        qkv = _matmul(h, w_na, BF16, 1024, 1024, name="na_proj")
        lat = _matmul(h, w_lat, F32, 512, w_lat.shape[1], name="mla_latent_proj")
        q_cat = _mla_q(lat, q_norm[l], wq, cos, sin, mla_heads, seq)
        k_cat, v_mla = _mla_kv(lat, ckv_blk, kr_blk, kv_norm[l], wk, wv, cos, sin, mla_heads, seq)
        o_a = _mla_attn(q_cat, k_cat, v_mla, batch, seq, mla_heads)
        o_b = _na_attn(qkv, _na_toeplitz(na_rpb[l]), batch, seq, na_heads)
        merged = _gate_merge(h, w_gate[l].astype(BF16), b_gate[l].reshape(1, -1), o_a,
                             w_branch_mla[l].astype(BF16), o_b, w_branch_na[l].astype(BF16))
        xf = _matmul(merged, w_out[l].astype(BF16), F32, 512, 1024, residual=xf, name="out_proj")

        h2, h2t = _rmsnorm_cast(xf, ffn_norm[l], transposed=True)
        ph = peer_subkeys.shape[1]
        sk = peer_subkeys[l].reshape(2 * ph, N_KEYS, -1).astype(BF16)
        st = _peer_scores(h2, peer_w_query[l].astype(BF16), sk)
        e1, e2, theta, rank2 = _peer_topk(st)
        peer_t = _peer_main(h2t, peer_u[l], peer_v[l], theta, rank2, e1, e2)
        if l + 1 < depth:
            xf = xf + peer_t.T
    return _final(xf, peer_t, final_norm).reshape(batch, seq, d)
```
